```python
import functools
import jax, jax.numpy as jnp
from jax import lax
import numpy as np

D_MODEL = 2048
BATCH = 4
SEQ = 8192
DEPTH = 1
DEC_BATCH = 32
DEC_SEQ = 64
PAST_LEN = 4096

CHUNK = 64
N_PREV_CHUNKS = 8
BAND = N_PREV_CHUNKS * CHUNK
N_HEADS = 8
HEAD_DIM = 128
ATTN_WIDTH = N_HEADS * HEAD_DIM
CONV_CH = D_MODEL - ATTN_WIDTH
MIX_WIDTH = ATTN_WIDTH + CONV_CH
IN_WIDTH = 3 * ATTN_WIDTH + 3 * CONV_CH
REL_CLIP = 128
CONV_K = 3
N_MEM = 256
X_HEADS = 4
X_HEAD_DIM = 256
X_WIDTH = X_HEADS * X_HEAD_DIM
D_FF = 5632
EPS = 1e-6
NEG_INF = -1e30

kernel_name = "hybrid_streaming_encoder_step"


def rms_norm(x, g):
    xf = x.astype(jnp.float32)
    y = xf * lax.rsqrt(jnp.mean(xf * xf, axis=-1, keepdims=True) + EPS)
    return y.astype(x.dtype) * g


def causal_dwconv(x, prev, w):
    T = x.shape[1]
    xp = jnp.concatenate([prev.astype(x.dtype), x], axis=1)
    y = xp[:, 0:T] * w[0]
    for kk in range(1, CONV_K):
        y = y + xp[:, kk:kk + T] * w[kk]
    return y, xp[:, T:]


def rel_bias_lookup(table, dist):
    return table[:, jnp.clip(dist, -REL_CLIP, REL_CLIP) + REL_CLIP].astype(jnp.float32)


def biased_softmax(s, bias, valid):
    s = s.astype(jnp.float32) * (HEAD_DIM ** -0.5) + bias
    if valid is not None:
        s = jnp.where(valid, s, NEG_INF)
    return jax.nn.softmax(s, axis=-1)


def prompt_band_attention(q, k, v, table, keep):
    B, S = q.shape[:2]
    nc = S // CHUNK
    nb = N_PREV_CHUNKS + 1
    pad = ((0, 0), (BAND, 0), (0, 0), (0, 0))
    kp = jnp.pad(k, pad).reshape(B, nc + N_PREV_CHUNKS, CHUNK, N_HEADS, HEAD_DIM)
    vp = jnp.pad(v, pad).reshape(B, nc + N_PREV_CHUNKS, CHUNK, N_HEADS, HEAD_DIM)
    qc = q.reshape(B, nc, CHUNK, N_HEADS, HEAD_DIM)
    s = jnp.concatenate(
        [jnp.einsum("bcqhd,bckhd->bchqk", qc, kp[:, w:w + nc]) for w in range(nb)], axis=-1)
    qi = jnp.arange(CHUNK)
    kj = jnp.arange(nb * CHUNK)
    dist = BAND + qi[:, None] - kj[None, :]
    key_chunk = jnp.arange(nc)[:, None] - N_PREV_CHUNKS + (kj // CHUNK)[None, :]
    valid = (key_chunk >= 0)[None, :, None, None, :]
    p = biased_softmax(s, rel_bias_lookup(table, dist), valid).astype(v.dtype)
    p = p.reshape(B, nc, N_HEADS, CHUNK, nb, CHUNK)
    o = jnp.einsum("bchqk,bckhd->bcqhd", p[:, :, :, :, 0], vp[:, 0:nc])
    for w in range(1, nb):
        o = o + jnp.einsum("bchqk,bckhd->bcqhd", p[:, :, :, :, w], vp[:, w:w + nc])
    return (o.reshape(B, S, ATTN_WIDTH), k[:, S - keep:], v[:, S - keep:])


def sample_band_attention(q, k, v, cache_k, cache_v, table):
    B, T = q.shape[:2]
    L = cache_k.shape[1]
    k_all = jnp.concatenate([cache_k.astype(k.dtype), k], axis=1)
    v_all = jnp.concatenate([cache_v.astype(v.dtype), v], axis=1)
    dist = L + jnp.arange(T)[:, None] - jnp.arange(L + T)[None, :]
    s = jnp.einsum("bqhd,bkhd->bhqk", q, k_all)
    p = biased_softmax(s, rel_bias_lookup(table, dist), None).astype(v.dtype)
    o = jnp.einsum("bhqk,bkhd->bqhd", p, v_all)
    return (o.reshape(B, T, ATTN_WIDTH), k_all[:, T:], v_all[:, T:])


def memory_kv(mem, g_mem_norm, w_xkv, g_xk):
    B, M, _ = mem.shape
    mk, mv = jnp.split(rms_norm(mem, g_mem_norm) @ w_xkv, 2, axis=-1)
    mk = rms_norm(mk.reshape(B, M, X_HEADS, X_HEAD_DIM), g_xk)
    return mk, mv.reshape(B, M, X_HEADS, X_HEAD_DIM)


def cross_attention(h, mem_k, mem_v, g_norm2, w_xq, g_xq, w_xo):
    B, T, _ = h.shape
    q = rms_norm((rms_norm(h, g_norm2) @ w_xq).reshape(B, T, X_HEADS, X_HEAD_DIM), g_xq)
    s = jnp.einsum("bqhd,bkhd->bhqk", q, mem_k.astype(q.dtype)).astype(jnp.float32)
    p = jax.nn.softmax(s * (X_HEAD_DIM ** -0.5), axis=-1).astype(h.dtype)
    o = jnp.einsum("bhqk,bkhd->bqhd", p, mem_v.astype(h.dtype))
    return o.reshape(B, T, X_WIDTH) @ w_xo


def encoder_layer(x, band_fn, conv_prev, ffn_prev, mem_k, mem_v,
                  g_norm1, w_in, g_q, g_k, w_conv_mix, g_out_attn, g_out_conv, w_out,
                  g_norm2, w_xq, g_xq, w_xo, g_norm3, w_up, w_gate, w_ffn_conv, w_down):
    B, T, _ = x.shape
    n = rms_norm(x, g_norm1)
    cuts = [ATTN_WIDTH, 2 * ATTN_WIDTH, 3 * ATTN_WIDTH,
            3 * ATTN_WIDTH + CONV_CH, 3 * ATTN_WIDTH + 2 * CONV_CH]
    q, k, v, b_gate, c_gate, u = jnp.split(n @ w_in, cuts, axis=-1)
    q = rms_norm(q.reshape(B, T, N_HEADS, HEAD_DIM), g_q)
    k = rms_norm(k.reshape(B, T, N_HEADS, HEAD_DIM), g_k)
    v = v.reshape(B, T, N_HEADS, HEAD_DIM)
    a_out, new_k, new_v = band_fn(q, k, v)
    c_conv, new_conv = causal_dwconv(c_gate * u, conv_prev, w_conv_mix)
    mixed = jnp.concatenate([rms_norm(a_out, g_out_attn),
                             rms_norm(b_gate * c_conv, g_out_conv)], axis=-1)
    h = x + mixed @ w_out
    h = h + cross_attention(h, mem_k, mem_v, g_norm2, w_xq, g_xq, w_xo)
    n3 = rms_norm(h, g_norm3)
    up, new_ffn = causal_dwconv(n3 @ w_up, ffn_prev, w_ffn_conv)
    y = h + (jax.nn.silu(up) * (n3 @ w_gate)) @ w_down
    return y, new_k, new_v, new_conv, new_ffn


def setup_inputs(seed: int = 0) -> dict:
    key = jax.random.key(seed)
    ks = iter(jax.random.split(key, 40))

    def nrm(shape, scale=1.0):
        return scale * jax.random.normal(next(ks), shape, jnp.float32)

    def gain(shape):
        return 1.0 + nrm(shape, 0.02)

    L = min(BAND, PAST_LEN)
    return {
        "x_prompt": nrm((BATCH, SEQ, D_MODEL)),
        "x_sample": nrm((DEC_BATCH, DEC_SEQ, D_MODEL)),
        "mem_prompt": nrm((BATCH, N_MEM, D_MODEL)),
        "cache_attn_k": nrm((DEPTH, DEC_BATCH, L, N_HEADS, HEAD_DIM)),
        "cache_attn_v": nrm((DEPTH, DEC_BATCH, L, N_HEADS, HEAD_DIM)),
        "cache_conv": nrm((DEPTH, DEC_BATCH, CONV_K - 1, CONV_CH)),
        "cache_ffn_conv": nrm((DEPTH, DEC_BATCH, CONV_K - 1, D_FF)),
        "cache_mem_k": nrm((DEPTH, DEC_BATCH, N_MEM, X_HEADS, X_HEAD_DIM)),
        "cache_mem_v": nrm((DEPTH, DEC_BATCH, N_MEM, X_HEADS, X_HEAD_DIM)),
        "g_norm1": gain((DEPTH, D_MODEL)),
        "w_in": nrm((DEPTH, D_MODEL, IN_WIDTH), D_MODEL ** -0.5),
        "g_q": gain((DEPTH, HEAD_DIM)),
        "g_k": gain((DEPTH, HEAD_DIM)),
        "rel_bias": nrm((DEPTH, N_HEADS, 2 * REL_CLIP + 1), 0.1),
        "w_conv_mix": nrm((DEPTH, CONV_K, CONV_CH), CONV_K ** -0.5),
        "g_out_attn": gain((DEPTH, ATTN_WIDTH)),
        "g_out_conv": gain((DEPTH, CONV_CH)),
        "w_out": nrm((DEPTH, MIX_WIDTH, D_MODEL), MIX_WIDTH ** -0.5),
        "g_norm2": gain((DEPTH, D_MODEL)),
        "g_mem_norm": gain((DEPTH, D_MODEL)),
        "w_xq": nrm((DEPTH, D_MODEL, X_WIDTH), D_MODEL ** -0.5),
        "w_xkv": nrm((DEPTH, D_MODEL, 2 * X_WIDTH), D_MODEL ** -0.5),
        "g_xq": gain((DEPTH, X_HEAD_DIM)),
        "g_xk": gain((DEPTH, X_HEAD_DIM)),
        "w_xo": nrm((DEPTH, X_WIDTH, D_MODEL), X_WIDTH ** -0.5),
        "g_norm3": gain((DEPTH, D_MODEL)),
        "w_up": nrm((DEPTH, D_MODEL, D_FF), D_MODEL ** -0.5),
        "w_gate": nrm((DEPTH, D_MODEL, D_FF), D_MODEL ** -0.5),
        "w_ffn_conv": nrm((DEPTH, CONV_K, D_FF), CONV_K ** -0.5),
        "w_down": nrm((DEPTH, D_FF, D_MODEL), D_FF ** -0.5),
    }


def reference(x_prompt, x_sample, mem_prompt, cache_attn_k, cache_attn_v, cache_conv,
              cache_ffn_conv, cache_mem_k, cache_mem_v, g_norm1, w_in, g_q, g_k, rel_bias,
              w_conv_mix, g_out_attn, g_out_conv, w_out, g_norm2, g_mem_norm, w_xq, w_xkv,
              g_xq, g_xk, w_xo, g_norm3, w_up, w_gate, w_ffn_conv, w_down):
    Bp, Sp, _ = x_prompt.shape
    keep = min(BAND, Sp)
    yp, ys = x_prompt, x_sample
    p_k, p_v, p_c, p_f, p_mk, p_mv = [], [], [], [], [], []
    s_k, s_v, s_c, s_f = [], [], [], []
    for l in range(DEPTH):
        layer_w = (g_norm1[l], w_in[l], g_q[l], g_k[l], w_conv_mix[l], g_out_attn[l],
                   g_out_conv[l], w_out[l], g_norm2[l], w_xq[l], g_xq[l], w_xo[l],
                   g_norm3[l], w_up[l], w_gate[l], w_ffn_conv[l], w_down[l])
        mk_p, mv_p = memory_kv(mem_prompt, g_mem_norm[l], w_xkv[l], g_xk[l])
        prompt_band = functools.partial(prompt_band_attention, table=rel_bias[l], keep=keep)
        zc = jnp.zeros((Bp, CONV_K - 1, CONV_CH), yp.dtype)
        zf = jnp.zeros((Bp, CONV_K - 1, D_FF), yp.dtype)
        yp, pk, pv, pc, pf = encoder_layer(yp, prompt_band, zc, zf, mk_p, mv_p, *layer_w)
        sample_band = functools.partial(sample_band_attention, cache_k=cache_attn_k[l],
                                        cache_v=cache_attn_v[l], table=rel_bias[l])
        ys, sk, sv, sc, sf = encoder_layer(ys, sample_band, cache_conv[l], cache_ffn_conv[l],
                                           cache_mem_k[l], cache_mem_v[l], *layer_w)
        p_k.append(pk); p_v.append(pv); p_c.append(pc); p_f.append(pf)
        p_mk.append(mk_p); p_mv.append(mv_p)
        s_k.append(sk); s_v.append(sv); s_c.append(sc); s_f.append(sf)
    return (yp, ys,
            jnp.stack(p_k), jnp.stack(p_v), jnp.stack(p_c), jnp.stack(p_f),
            jnp.stack(p_mk), jnp.stack(p_mv),
            jnp.stack(s_k), jnp.stack(s_v), jnp.stack(s_c), jnp.stack(s_f))
```

```python
import functools

import jax
import jax.numpy as jnp
import numpy as np
from jax import lax
from jax.experimental import pallas as pl
from jax.experimental.pallas import tpu as pltpu

CHUNK = 64
N_PREV_CHUNKS = 8
BAND = N_PREV_CHUNKS * CHUNK
N_HEADS = 8
HEAD_DIM = 128
ATTN_WIDTH = N_HEADS * HEAD_DIM
REL_CLIP = 128
CONV_K = 3
X_HEADS = 4
X_HEAD_DIM = 256
X_WIDTH = X_HEADS * X_HEAD_DIM
EPS = 1e-6
NEG_INF = -1e30

BF16 = jnp.bfloat16
F32 = jnp.float32

V7X_VMEM_BYTES = 64 * 1024 * 1024
BF16_SUBLANES = 16

ATTN_TQ = 4 * CHUNK
ATTN_PIECES = (BAND + ATTN_TQ) // ATTN_TQ


def _nbytes(shape, dtype):
    return int(np.prod(shape)) * jnp.dtype(dtype).itemsize


def _vmem_limit(pipelined, resident=(), temps=0):
    est = 2 * sum(_nbytes(s, d) for s, d in pipelined)
    est += sum(_nbytes(s, d) for s, d in resident) + temps
    return min(int(est * 1.25) + (2 << 20), V7X_VMEM_BYTES - (6 << 20))


def _rms(xf, g):
    return xf * lax.rsqrt(jnp.mean(xf * xf, axis=-1, keepdims=True) + EPS) * g


def _dot(a, b):
    return jnp.dot(a, b, preferred_element_type=F32)


def _dot_nt(a, b):
    return lax.dot_general(a, b, (((1,), (1,)), ((), ())), preferred_element_type=F32)


def _proj_kernel(x_ref, gin_ref, w_ref, gcol_ref, *rest, head_dim, n_norm_tiles, aux_lo, aux_hi):
    has_aux = aux_hi > aux_lo
    if has_aux:
        out_ref, aux_ref, n_scr = rest
    else:
        out_ref, n_scr = rest
        aux_ref = None
    j = pl.program_id(1)
    tn = out_ref.shape[1]

    @pl.when(j == 0)
    def _():
        n_scr[...] = _rms(x_ref[...], gin_ref[...]).astype(BF16)

    acc = _dot(n_scr[...], w_ref[...])
    in_aux = (j >= aux_lo) & (j < aux_hi)

    @pl.when(j < n_norm_tiles)
    def _():
        for h in range(tn // head_dim):
            sl = slice(h * head_dim, (h + 1) * head_dim)
            y = _rms(acc[:, sl], gcol_ref[:, sl])
            out_ref[:, sl] = y.astype(out_ref.dtype)
            if has_aux:
                @pl.when(in_aux)
                def _():
                    aux_ref[:, sl] = y

    @pl.when(j >= n_norm_tiles)
    def _():
        out_ref[...] = acc.astype(out_ref.dtype)
        if has_aux:
            @pl.when(in_aux)
            def _():
                aux_ref[...] = acc


def _proj(x, g_in, w, gcol, *, tm, tn, head_dim, norm_cols, out_dtype, aux_cols=None, name):
    t, d = x.shape
    n = w.shape[1]
    has_aux = aux_cols is not None
    aux_lo, aux_hi = (aux_cols[0] // tn, aux_cols[1] // tn) if has_aux else (0, 0)
    out_shape = [jax.ShapeDtypeStruct((t, n), out_dtype)]
    out_specs = [pl.BlockSpec((tm, tn), lambda i, j: (i, j))]
    pipelined = [((tm, d), F32), ((d, tn), BF16), ((tm, tn), out_dtype)]
    if has_aux:
        n_aux = aux_hi - aux_lo
        out_shape.append(jax.ShapeDtypeStruct((t, n_aux * tn), F32))
        out_specs.append(pl.BlockSpec(
            (tm, tn), lambda i, j: (i, jnp.clip(j - aux_lo, 0, n_aux - 1))))
        pipelined.append(((tm, tn), F32))
    kern = functools.partial(_proj_kernel, head_dim=head_dim, n_norm_tiles=norm_cols // tn,
                             aux_lo=aux_lo, aux_hi=aux_hi)
    return pl.pallas_call(
        kern,
        grid=(t // tm, n // tn),
        in_specs=[
            pl.BlockSpec((tm, d), lambda i, j: (i, 0)),
            pl.BlockSpec((1, d), lambda i, j: (0, 0)),
            pl.BlockSpec((d, tn), lambda i, j: (0, j)),
            pl.BlockSpec((1, tn), lambda i, j: (0, j)),
        ],
        out_specs=out_specs,
        out_shape=out_shape,
        scratch_shapes=[pltpu.VMEM((tm, d), BF16)],
        compiler_params=pltpu.CompilerParams(
            dimension_semantics=("arbitrary", "arbitrary"),
            vmem_limit_bytes=_vmem_limit(pipelined, [((tm, d), BF16)],
                                         temps=3 * _nbytes((tm, tn), F32) + _nbytes((tm, d), F32))),
        name=name,
    )(x, g_in, w, gcol)


def _softmax_pv(scores, values):
    m = functools.reduce(jnp.maximum, [jnp.max(s, axis=-1, keepdims=True) for s in scores])
    ps = [jnp.exp(s - m) for s in scores]
    l = functools.reduce(jnp.add, [jnp.sum(p, axis=-1, keepdims=True) for p in ps])
    o = functools.reduce(jnp.add, [_dot(p.astype(BF16), v) for p, v in zip(ps, values)])
    return o * (1.0 / l)


def _prompt_attn_kernel(q_ref, *rest):
    k_refs = rest[:ATTN_PIECES]
    v_refs = rest[ATTN_PIECES:2 * ATTN_PIECES]
    bias_ref, o_ref = rest[2 * ATTN_PIECES:]
    t = pl.program_id(1)
    scale = HEAD_DIM ** -0.5
    pens = [jnp.where(t - (ATTN_PIECES - 1) + w >= 0, 0.0, NEG_INF).astype(F32)
            for w in range(ATTN_PIECES - 1)]
    for h in range(N_HEADS):
        sl = slice(h * HEAD_DIM, (h + 1) * HEAD_DIM)
        q = q_ref[0, :, sl]
        scores = []
        for w in range(ATTN_PIECES):
            s = _dot_nt(q, k_refs[w][0, :, sl]) * scale + bias_ref[h, w]
            if w < ATTN_PIECES - 1:
                s = s + pens[w]
            scores.append(s)
        o = _softmax_pv(scores, [v_refs[w][0, :, sl] for w in range(ATTN_PIECES)])
        o_ref[0, :, sl] = o.astype(o_ref.dtype)


def _prompt_attn(proj, bias, *, batch, seq):
    tq = ATTN_TQ
    np_ = ATTN_PIECES

    def kv_spec(w, col):
        return pl.BlockSpec((1, tq, ATTN_WIDTH),
                            lambda b, t: (b, jnp.maximum(t - (np_ - 1) + w, 0), col))

    blk = ((1, tq, ATTN_WIDTH), BF16)
    bias_blk = ((N_HEADS, np_, tq, tq), F32)
    return pl.pallas_call(
        _prompt_attn_kernel,
        grid=(batch, seq // tq),
        in_specs=[pl.BlockSpec((1, tq, ATTN_WIDTH), lambda b, t: (b, t, 0))]
        + [kv_spec(w, 1) for w in range(np_)]
        + [kv_spec(w, 2) for w in range(np_)]
        + [pl.BlockSpec((N_HEADS, np_, tq, tq), lambda b, t: (0, 0, 0, 0))],
        out_specs=pl.BlockSpec((1, tq, ATTN_WIDTH), lambda b, t: (b, t, 0)),
        out_shape=jax.ShapeDtypeStruct((batch, seq, ATTN_WIDTH), BF16),
        compiler_params=pltpu.CompilerParams(
            dimension_semantics=("arbitrary", "arbitrary"),
            vmem_limit_bytes=_vmem_limit([blk] * (2 + 2 * np_) + [bias_blk],
                                         temps=12 * _nbytes((tq, tq), F32))),
        name="prompt_band_attn",
    )(proj, *([proj] * (2 * np_)), bias)


def _sample_attn_kernel(q_ref, kn_ref, vn_ref, kf_ref, vf_ref, ck_ref, cv_ref, bc_ref, bn_ref,
                        o_ref, sk_ref, sv_ref):
    scale = HEAD_DIM ** -0.5
    t_new = q_ref.shape[1]
    keep = ck_ref.shape[1] - t_new
    for h in range(N_HEADS):
        sl = slice(h * HEAD_DIM, (h + 1) * HEAD_DIM)
        q = q_ref[0, :, sl]
        s_c = _dot_nt(q, ck_ref[0, :, sl].astype(BF16)) * scale + bc_ref[h]
        s_n = _dot_nt(q, kn_ref[0, :, sl]) * scale + bn_ref[h]
        o = _softmax_pv([s_c, s_n], [cv_ref[0, :, sl].astype(BF16), vn_ref[0, :, sl]])
        o_ref[0, :, sl] = o.astype(o_ref.dtype)
    sk_ref[0, :keep, :] = ck_ref[0, t_new:, :]
    sk_ref[0, keep:, :] = kf_ref[0]
    sv_ref[0, :keep, :] = cv_ref[0, t_new:, :]
    sv_ref[0, keep:, :] = vf_ref[0]


def _sample_attn(proj, kv_f32, cache_k, cache_v, bias_c, bias_n):
    b, t, _ = proj.shape
    l = cache_k.shape[1]
    new_bf = ((1, t, ATTN_WIDTH), BF16)
    new_f = ((1, t, ATTN_WIDTH), F32)
    cache_blk = ((1, l, ATTN_WIDTH), F32)

    def new_spec(col):
        return pl.BlockSpec((1, t, ATTN_WIDTH), lambda i: (i, 0, col))

    cache_spec = pl.BlockSpec((1, l, ATTN_WIDTH), lambda i: (i, 0, 0))
    return pl.pallas_call(
        _sample_attn_kernel,
        grid=(b,),
        in_specs=[new_spec(0), new_spec(1), new_spec(2), new_spec(0), new_spec(1),
                  cache_spec, cache_spec,
                  pl.BlockSpec(bias_c.shape, lambda i: (0, 0, 0)),
                  pl.BlockSpec(bias_n.shape, lambda i: (0, 0, 0))],
        out_specs=[new_spec(0), cache_spec, cache_spec],
        out_shape=[jax.ShapeDtypeStruct((b, t, ATTN_WIDTH), BF16),
                   jax.ShapeDtypeStruct((b, l, ATTN_WIDTH), F32),
                   jax.ShapeDtypeStruct((b, l, ATTN_WIDTH), F32)],
        compiler_params=pltpu.CompilerParams(
            dimension_semantics=("arbitrary",),
            vmem_limit_bytes=_vmem_limit(
                [new_bf] * 4 + [new_f] * 2 + [cache_blk] * 4
                + [(bias_c.shape, F32), (bias_n.shape, F32)],
                temps=8 * _nbytes((t, l), F32) + 2 * _nbytes((l, HEAD_DIM), F32))),
        name="sample_band_attn",
    )(proj, proj, proj, kv_f32, kv_f32, cache_k, cache_v, bias_c, bias_n)


def _causal_dwconv3(x, prev, w_ref, nseg, lseg):
    c = x.shape[1]
    pos = lax.broadcasted_iota(jnp.int32, (nseg, lseg, c), 1)
    p0 = prev[:, 0:1, :]
    p1 = prev[:, 1:2, :]
    x3 = x.reshape(nseg, lseg, c)
    sh1 = pltpu.roll(x, 1, 0).reshape(nseg, lseg, c)
    sh2 = pltpu.roll(x, 2, 0).reshape(nseg, lseg, c)
    sh1 = jnp.where(pos == 0, p1, sh1)
    sh2 = jnp.where(pos == 0, p0, jnp.where(pos == 1, p1, sh2))
    y = sh2 * w_ref[0:1, :] + sh1 * w_ref[1:2, :] + x3 * w_ref[2:3, :]
    return y.reshape(nseg * lseg, c), x3[:, lseg - (CONV_K - 1):, :]


def _mid_kernel(a_ref, b_ref, c_ref, u_ref, ch_ref, uh_ref, prev_ref, x_ref, mk_ref, mv_ref,
                wconv_ref, ga_ref, gc_ref, wout_ref, g2_ref, wxq_ref, gxq_ref, wxo_ref,
                h_ref, newc_ref, mixed_scr, o_scr, *, nseg, lseg, tiles_per_seq):
    i = pl.program_id(0)
    tm = x_ref.shape[0]

    mixed_scr[:, :ATTN_WIDTH] = _rms(a_ref[...].astype(F32), ga_ref[...]).astype(BF16)

    cu = c_ref[...].astype(F32) * u_ref[...].astype(F32)
    if tiles_per_seq > 1:
        halo = (ch_ref[BF16_SUBLANES - 2:, :].astype(F32)
                * uh_ref[BF16_SUBLANES - 2:, :].astype(F32))
        prev = jnp.where(i % tiles_per_seq == 0, prev_ref[...], halo[None])
    else:
        prev = prev_ref[...]
    conv, new_conv = _causal_dwconv3(cu, prev, wconv_ref, nseg, lseg)
    newc_ref[...] = new_conv
    mixed_scr[:, ATTN_WIDTH:] = _rms(b_ref[...].astype(F32) * conv, gc_ref[...]).astype(BF16)

    h = x_ref[...] + _dot(mixed_scr[...], wout_ref[...])

    qx = _dot(_rms(h, g2_ref[...]).astype(BF16), wxq_ref[...])
    scale = X_HEAD_DIM ** -0.5
    for hd in range(X_HEADS):
        sl = slice(hd * X_HEAD_DIM, (hd + 1) * X_HEAD_DIM)
        qh = _rms(qx[:, sl], gxq_ref[...]).astype(BF16)
        for s in range(nseg):
            rows = slice(s * lseg, (s + 1) * lseg)
            sc = _dot_nt(qh[rows], mk_ref[s, :, sl].astype(BF16)) * scale
            o = _softmax_pv([sc], [mv_ref[s, :, sl].astype(BF16)])
            o_scr[rows, sl] = o.astype(BF16)

    h_ref[...] = h + _dot(o_scr[...], wxo_ref[...])


def _mid(a, proj, conv_prev, x, mem_k, mem_v, w_conv, g_a, g_c, w_out, g2, w_xq, g_xq, w_xo,
         *, tm, seq_len):
    t, d = x.shape
    cw = conv_prev.shape[2]
    n_mem = mem_k.shape[1]
    lseg = min(tm, seq_len)
    nseg = tm // lseg
    tps = max(1, seq_len // tm)
    halo_rows = BF16_SUBLANES
    halo_per_tile = tm // halo_rows

    def col_spec(col):
        return pl.BlockSpec((tm, cw), lambda i: (i, col))

    def halo_spec(col):
        return pl.BlockSpec((halo_rows, cw),
                            lambda i: (jnp.maximum(i * halo_per_tile - 1, 0), col))

    def const_spec(arr):
        return pl.BlockSpec(arr.shape, lambda i: (0,) * arr.ndim, pipeline_mode=pl.Buffered(1))

    seq_spec3 = lambda rows, width: pl.BlockSpec((nseg, rows, width), lambda i: (i // tps, 0, 0))
    kern = functools.partial(_mid_kernel, nseg=nseg, lseg=lseg, tiles_per_seq=tps)
    pipelined = ([((tm, cw), BF16)] * 4 + [((halo_rows, cw), BF16)] * 2
                 + [((nseg, 2, cw), F32)] * 2 + [((tm, d), F32)] * 2
                 + [((nseg, n_mem, X_WIDTH), F32)] * 2)
    resident = [(w.shape, w.dtype) for w in (w_out, w_xq, w_xo)]
    resident += [((tm, d), BF16), ((tm, X_WIDTH), BF16)]
    return pl.pallas_call(
        kern,
        grid=(t // tm,),
        in_specs=[
            pl.BlockSpec((tm, ATTN_WIDTH), lambda i: (i, 0)),
            col_spec(3), col_spec(4), col_spec(5), halo_spec(4), halo_spec(5),
            seq_spec3(2, cw),
            pl.BlockSpec((tm, d), lambda i: (i, 0)),
            seq_spec3(n_mem, X_WIDTH), seq_spec3(n_mem, X_WIDTH),
            const_spec(w_conv), const_spec(g_a), const_spec(g_c), const_spec(w_out),
            const_spec(g2), const_spec(w_xq), const_spec(g_xq), const_spec(w_xo),
        ],
        out_specs=[pl.BlockSpec((tm, d), lambda i: (i, 0)), seq_spec3(2, cw)],
        out_shape=[jax.ShapeDtypeStruct((t, d), F32),
                   jax.ShapeDtypeStruct(conv_prev.shape, F32)],
        scratch_shapes=[pltpu.VMEM((tm, d), BF16), pltpu.VMEM((tm, X_WIDTH), BF16)],
        compiler_params=pltpu.CompilerParams(
            dimension_semantics=("arbitrary",),
            vmem_limit_bytes=_vmem_limit(pipelined, resident, temps=8 * _nbytes((tm, d), F32))),
        name="mid",
    )(a, proj, proj, proj, proj, proj, conv_prev, x, mem_k, mem_v,
      w_conv, g_a, g_c, w_out, g2, w_xq, g_xq, w_xo)


def _ffn_kernel(h_ref, g3_ref, wup_ref, wgate_ref, wconv_ref, wdown_ref, prev_ref,
                y_ref, newf_ref, n3_scr, carry_scr, *, nseg, lseg, tiles_per_seq):
    i = pl.program_id(0)
    f = pl.program_id(1)
    tm = h_ref.shape[0]

    @pl.when(f == 0)
    def _():
        h = h_ref[...]
        n3_scr[...] = _rms(h, g3_ref[...]).astype(BF16)
        y_ref[...] = h

    n3 = n3_scr[...]
    up = _dot(n3, wup_ref[...])
    gate = _dot(n3, wgate_ref[...])
    if tiles_per_seq > 1:
        prev = jnp.where(i % tiles_per_seq == 0, prev_ref[...], carry_scr[f][None])
        carry_scr[f] = up[tm - (CONV_K - 1):, :]
    else:
        prev = prev_ref[...]
    conv, new_ffn = _causal_dwconv3(up, prev, wconv_ref, nseg, lseg)
    newf_ref[...] = new_ffn
    act = conv / (1.0 + jnp.exp(-conv)) * gate
    y_ref[...] += _dot(act.astype(BF16), wdown_ref[...])


def _ffn(h, g3, w_up, w_gate, w_conv, w_down, ffn_prev, *, tm, tf, seq_len):
    t, d = h.shape
    dff = w_up.shape[1]
    lseg = min(tm, seq_len)
    nseg = tm // lseg
    tps = max(1, seq_len // tm)
    nf = dff // tf
    kern = functools.partial(_ffn_kernel, nseg=nseg, lseg=lseg, tiles_per_seq=tps)
    prev_spec = pl.BlockSpec((nseg, CONV_K - 1, tf), lambda i, f: (i // tps, 0, f))
    pipelined = ([((tm, d), F32)] * 2 + [((d, tf), BF16)] * 3
                 + [((nseg, 2, tf), F32)] * 2 + [((CONV_K, tf), F32)])
    resident = [((tm, d), BF16), ((nf, CONV_K - 1, tf), F32)]
    return pl.pallas_call(
        kern,
        grid=(t // tm, nf),
        in_specs=[
            pl.BlockSpec((tm, d), lambda i, f: (i, 0)),
            pl.BlockSpec((1, d), lambda i, f: (0, 0)),
            pl.BlockSpec((d, tf), lambda i, f: (0, f)),
            pl.BlockSpec((d, tf), lambda i, f: (0, f)),
            pl.BlockSpec((CONV_K, tf), lambda i, f: (0, f)),
            pl.BlockSpec((tf, d), lambda i, f: (f, 0)),
            prev_spec,
        ],
        out_specs=[pl.BlockSpec((tm, d), lambda i, f: (i, 0)), prev_spec],
        out_shape=[jax.ShapeDtypeStruct((t, d), F32),
                   jax.ShapeDtypeStruct(ffn_prev.shape, F32)],
        scratch_shapes=[pltpu.VMEM((tm, d), BF16), pltpu.VMEM((nf, CONV_K - 1, tf), F32)],
        compiler_params=pltpu.CompilerParams(
            dimension_semantics=("arbitrary", "arbitrary"),
            vmem_limit_bytes=_vmem_limit(pipelined, resident, temps=10 * _nbytes((tm, tf), F32))),
        name="ffn",
    )(h, g3, w_up, w_gate, w_conv, w_down, ffn_prev)


def _rel_index(n_q, n_k):
    dist = BAND + np.arange(n_q)[:, None] - np.arange(n_k)[None, :]
    return np.clip(dist, -REL_CLIP, REL_CLIP) + REL_CLIP


def _prompt_bias(table):
    nk = BAND + ATTN_TQ
    idx = _rel_index(ATTN_TQ, nk)
    qc = np.arange(ATTN_TQ)[:, None] // CHUNK
    kc = np.arange(nk)[None, :] // CHUNK
    in_band = (kc >= qc) & (kc <= qc + N_PREV_CHUNKS)
    bias = jnp.where(in_band[None], table[:, idx].astype(F32), NEG_INF)
    return bias.reshape(N_HEADS, ATTN_TQ, ATTN_PIECES, ATTN_TQ).transpose(0, 2, 1, 3)


def _sample_bias(table, t_new, n_cache):
    bias = table[:, _rel_index(t_new, n_cache + t_new)].astype(F32)
    return bias[:, :, :n_cache], bias[:, :, n_cache:]


def _layer(x, *, seq_len, attn_fn, conv_prev, ffn_prev, mem_k, mem_v, w, tm_proj, tm_mid, tm_ffn):
    batch = x.shape[0] // seq_len
    proj, kv_f32 = _proj(x, w["g_norm1"], w["w_in"], w["g_in_cols"], tm=tm_proj, tn=512,
                         head_dim=HEAD_DIM, norm_cols=2 * ATTN_WIDTH, out_dtype=BF16,
                         aux_cols=(ATTN_WIDTH, 3 * ATTN_WIDTH), name="in_proj")
    a, new_k, new_v = attn_fn(proj, kv_f32)
    h, new_conv = _mid(a.reshape(-1, ATTN_WIDTH), proj, conv_prev, x, mem_k, mem_v,
                       w["w_conv_mix"], w["g_out_attn"], w["g_out_conv"], w["w_out"],
                       w["g_norm2"], w["w_xq"], w["g_xq"], w["w_xo"], tm=tm_mid, seq_len=seq_len)
    y, new_ffn = _ffn(h, w["g_norm3"], w["w_up"], w["w_gate"], w["w_ffn_conv"], w["w_down"],
                      ffn_prev, tm=tm_ffn, tf=512, seq_len=seq_len)
    del batch
    return y, new_k, new_v, new_conv, new_ffn


def kernel(x_prompt, x_sample, mem_prompt, cache_attn_k, cache_attn_v, cache_conv, cache_ffn_conv, cache_mem_k, cache_mem_v, g_norm1, w_in, g_q, g_k, rel_bias, w_conv_mix, g_out_attn, g_out_conv, w_out, g_norm2, g_mem_norm, w_xq, w_xkv, g_xq, g_xk, w_xo, g_norm3, w_up, w_gate, w_ffn_conv, w_down):
    bp, sp, d = x_prompt.shape
    bs, ss, _ = x_sample.shape
    depth = w_in.shape[0]
    in_width = w_in.shape[2]
    conv_ch = w_conv_mix.shape[2]
    d_ff = w_up.shape[2]
    n_mem = mem_prompt.shape[1]
    n_cache = cache_attn_k.shape[2]
    keep = min(BAND, sp)

    yp = x_prompt.reshape(bp * sp, d)
    ys = x_sample.reshape(bs * ss, d)
    outs = [[] for _ in range(10)]
    for l in range(depth):
        row = lambda g: g[l][None, :]
        w = dict(
            g_norm1=row(g_norm1), w_in=w_in[l].astype(BF16),
            g_in_cols=jnp.concatenate([jnp.tile(g_q[l], N_HEADS), jnp.tile(g_k[l], N_HEADS),
                                       jnp.ones((in_width - 2 * ATTN_WIDTH,), F32)])[None, :],
            w_conv_mix=w_conv_mix[l], g_out_attn=row(g_out_attn), g_out_conv=row(g_out_conv),
            w_out=w_out[l].astype(BF16), g_norm2=row(g_norm2), w_xq=w_xq[l].astype(BF16),
            g_xq=row(g_xq), w_xo=w_xo[l].astype(BF16), g_norm3=row(g_norm3),
            w_up=w_up[l].astype(BF16), w_gate=w_gate[l].astype(BF16),
            w_ffn_conv=w_ffn_conv[l], w_down=w_down[l].astype(BF16))

        g_mem_cols = jnp.concatenate([jnp.tile(g_xk[l], X_HEADS), jnp.ones((X_WIDTH,), F32)])[None, :]
        (mem_kv,) = _proj(mem_prompt.reshape(bp * n_mem, d), row(g_mem_norm), w_xkv[l].astype(BF16),
                          g_mem_cols, tm=bp * n_mem, tn=512, head_dim=X_HEAD_DIM,
                          norm_cols=X_WIDTH, out_dtype=F32, name="mem_kv")
        mem_kv = mem_kv.reshape(bp, n_mem, 2 * X_WIDTH)
        mk_p, mv_p = mem_kv[:, :, :X_WIDTH], mem_kv[:, :, X_WIDTH:]
        p_bias = _prompt_bias(rel_bias[l])

        def prompt_attn(proj, kv_f32):
            a = _prompt_attn(proj.reshape(bp, sp, in_width), p_bias, batch=bp, seq=sp)
            kv = kv_f32.reshape(bp, sp, 2 * ATTN_WIDTH)[:, sp - keep:]
            return a, kv[:, :, :ATTN_WIDTH], kv[:, :, ATTN_WIDTH:]

        yp, pk, pv, pc, pf = _layer(
            yp, seq_len=sp, attn_fn=prompt_attn,
            conv_prev=jnp.zeros((bp, CONV_K - 1, conv_ch), F32),
            ffn_prev=jnp.zeros((bp, CONV_K - 1, d_ff), F32),
            mem_k=mk_p, mem_v=mv_p, w=w, tm_proj=1024, tm_mid=256, tm_ffn=512)

        s_bias_c, s_bias_n = _sample_bias(rel_bias[l], ss, n_cache)
        ck = cache_attn_k[l].reshape(bs, n_cache, ATTN_WIDTH)
        cv = cache_attn_v[l].reshape(bs, n_cache, ATTN_WIDTH)

        def sample_attn(proj, kv_f32):
            return _sample_attn(proj.reshape(bs, ss, in_width),
                                kv_f32.reshape(bs, ss, 2 * ATTN_WIDTH), ck, cv, s_bias_c, s_bias_n)

        ys, sk, sv, sc, sf = _layer(
            ys, seq_len=ss, attn_fn=sample_attn, conv_prev=cache_conv[l], ffn_prev=cache_ffn_conv[l],
            mem_k=cache_mem_k[l].reshape(bs, n_mem, X_WIDTH),
            mem_v=cache_mem_v[l].reshape(bs, n_mem, X_WIDTH),
            w=w, tm_proj=1024, tm_mid=256, tm_ffn=512)

        for lst, val in zip(outs, (
                pk.reshape(bp, keep, N_HEADS, HEAD_DIM), pv.reshape(bp, keep, N_HEADS, HEAD_DIM),
                pc, pf,
                mk_p.reshape(bp, n_mem, X_HEADS, X_HEAD_DIM), mv_p.reshape(bp, n_mem, X_HEADS, X_HEAD_DIM),
                sk.reshape(bs, n_cache, N_HEADS, HEAD_DIM), sv.reshape(bs, n_cache, N_HEADS, HEAD_DIM),
                sc, sf)):
            lst.append(val)

    return (yp.reshape(bp, sp, d), ys.reshape(bs, ss, d)) + tuple(jnp.stack(o) for o in outs)
```

```python
import functools
import math

import jax
import jax.numpy as jnp
import numpy as np
from jax import lax
from jax.experimental import pallas as pl
from jax.experimental.pallas import tpu as pltpu

CHUNK = 64
N_PREV_CHUNKS = 8
BAND = N_PREV_CHUNKS * CHUNK
N_HEADS = 8
HEAD_DIM = 128
ATTN_WIDTH = N_HEADS * HEAD_DIM
REL_CLIP = 128
CONV_K = 3
X_HEADS = 4
X_HEAD_DIM = 256
X_WIDTH = X_HEADS * X_HEAD_DIM
EPS = 1e-6
NEG_INF = -1e30
LOG2E = math.log2(math.e)

BF16 = jnp.bfloat16
F32 = jnp.float32

V7X_VMEM_BYTES = 64 * 1024 * 1024
BF16_SUBLANES = 16

KEY_PIECE = 2 * REL_CLIP
ATTN_PIECES = BAND // KEY_PIECE + 1
assert KEY_PIECE % CHUNK == 0 and BAND % KEY_PIECE == 0


def _nbytes(shape, dtype):
    return int(np.prod(shape)) * jnp.dtype(dtype).itemsize


def _vmem_limit(pipelined, resident=(), temps=0):
    est = 2 * sum(_nbytes(s, d) for s, d in pipelined)
    est += sum(_nbytes(s, d) for s, d in resident) + temps
    return min(int(est * 1.25) + (2 << 20), V7X_VMEM_BYTES - (6 << 20))


def _rms(xf, g):
    return xf * lax.rsqrt(jnp.mean(xf * xf, axis=-1, keepdims=True) + EPS) * g


def _dot(a, b):
    return jnp.dot(a, b, preferred_element_type=F32)


def _dot_nt(a, b):
    return lax.dot_general(a, b, (((1,), (1,)), ((), ())), preferred_element_type=F32)


def _lookup(j, table):
    return functools.reduce(lambda acc, kv: jnp.where(j == kv[0], kv[1], acc),
                            enumerate(table), jnp.int32(0))


def _proj_kernel(x_ref, gin_ref, w_ref, gcol_ref, *rest, head_dim, norm_flags, has_aux):
    if has_aux:
        out_ref, aux_ref, n_scr = rest
    else:
        out_ref, n_scr = rest
    j = pl.program_id(1)
    tn = out_ref.shape[1]

    @pl.when(j == 0)
    def _():
        n_scr[...] = _rms(x_ref[...], gin_ref[...]).astype(BF16)

    acc = _dot(n_scr[...], w_ref[...])
    is_norm = _lookup(j, norm_flags) != 0
    for h in range(tn // head_dim):
        sl = slice(h * head_dim, (h + 1) * head_dim)
        a = acc[:, sl]
        r = lax.rsqrt(jnp.mean(a * a, axis=-1, keepdims=True) + EPS)
        y = a * jnp.where(is_norm, r, 1.0) * gcol_ref[:, sl]
        out_ref[:, sl] = y.astype(out_ref.dtype)
        if has_aux:
            aux_ref[:, sl] = y


def _proj(x, g_in, w, gcol, *, tm, tn, head_dim, norm_cols, out_dtype, aux_cols=None, name):
    t, d = x.shape
    n = w.shape[1]
    nj = n // tn
    has_aux = aux_cols is not None
    aux_tiles = tuple(range(aux_cols[0] // tn, aux_cols[1] // tn)) if has_aux else ()
    order = tuple(c for c in range(nj) if c not in aux_tiles) + aux_tiles
    norm_flags = tuple(int(c < norm_cols // tn) for c in order)
    n_aux = len(aux_tiles)
    col = lambda j: _lookup(j, order)
    out_shape = [jax.ShapeDtypeStruct((t, n), out_dtype)]
    out_specs = [pl.BlockSpec((tm, tn), lambda i, j: (i, col(j)))]
    pipelined = [((tm, d), F32), ((d, tn), BF16), ((tm, tn), out_dtype)]
    if has_aux:
        out_shape.append(jax.ShapeDtypeStruct((t, n_aux * tn), F32))
        out_specs.append(pl.BlockSpec(
            (tm, tn), lambda i, j: (i, jnp.maximum(j - (nj - n_aux), 0))))
        pipelined.append(((tm, tn), F32))
    kern = functools.partial(_proj_kernel, head_dim=head_dim, norm_flags=norm_flags,
                             has_aux=has_aux)
    return pl.pallas_call(
        kern,
        grid=(t // tm, nj),
        in_specs=[
            pl.BlockSpec((tm, d), lambda i, j: (i, 0)),
            pl.BlockSpec((1, d), lambda i, j: (0, 0)),
            pl.BlockSpec((d, tn), lambda i, j: (0, col(j))),
            pl.BlockSpec((1, tn), lambda i, j: (0, col(j))),
        ],
        out_specs=out_specs,
        out_shape=out_shape,
        scratch_shapes=[pltpu.VMEM((tm, d), BF16)],
        compiler_params=pltpu.CompilerParams(
            dimension_semantics=("arbitrary", "arbitrary"),
            vmem_limit_bytes=_vmem_limit(pipelined, [((tm, d), BF16)],
                                         temps=3 * _nbytes((tm, tn), F32) + _nbytes((tm, d), F32))),
        name=name,
    )(x, g_in, w, gcol)


def _toeplitz(vec, rows):
    x = jnp.broadcast_to(vec, (rows, KEY_PIECE))
    r = lax.broadcasted_iota(jnp.int32, (rows, KEY_PIECE), 0)
    for b in range((rows - 1).bit_length()):
        x = jnp.where((r >> b) & 1 == 1, pltpu.roll(x, 1 << b, 1), x)
    return x


def _build_band_bias(pq_ref, bias_scr, mask_band):
    rows = bias_scr.shape[1]
    r = lax.broadcasted_iota(jnp.int32, (rows, KEY_PIECE), 0)
    c = lax.broadcasted_iota(jnp.int32, (rows, KEY_PIECE), 1)
    upper = c >= r
    for h in range(N_HEADS):
        top = jnp.broadcast_to(pq_ref[h, 0:1, :], (rows, KEY_PIECE))
        tp = _toeplitz(pq_ref[h, 1:2, :], rows)
        tq = _toeplitz(pq_ref[h, 2:3, :], rows)
        pieces = [top] * (ATTN_PIECES - 2) + [jnp.where(upper, tp, top), jnp.where(upper, tq, tp)]
        for w, piece in enumerate(pieces):
            piece = piece * LOG2E
            if mask_band:
                kc = w * (KEY_PIECE // CHUNK) + c // CHUNK
                qc = r // CHUNK
                piece = jnp.where((kc >= qc) & (kc <= qc + N_PREV_CHUNKS), piece, NEG_INF)
            bias_scr[h, :, w * KEY_PIECE:(w + 1) * KEY_PIECE] = piece


def _rel_rows(table):
    h = table.shape[0]
    top = table[:, 2 * REL_CLIP:]
    lo = table[:, :1]
    p = jnp.concatenate([jnp.broadcast_to(top, (h, REL_CLIP + 1)),
                         table[:, 2 * REL_CLIP - 1:REL_CLIP:-1]], axis=1)
    q = jnp.concatenate([table[:, REL_CLIP::-1],
                         jnp.broadcast_to(lo, (h, KEY_PIECE - REL_CLIP - 1))], axis=1)
    return jnp.stack([jnp.broadcast_to(top, (h, KEY_PIECE)), p, q], axis=1).astype(F32)


def _softmax_pv(scores, values):
    widths = {s.shape[1] for s in scores}
    if len(widths) == 1:
        m = jnp.max(functools.reduce(jnp.maximum, scores), axis=-1, keepdims=True)
    else:
        m = functools.reduce(jnp.maximum, [jnp.max(s, axis=-1, keepdims=True) for s in scores])
    ps = [jnp.exp2(s - m) for s in scores]
    if len(widths) == 1:
        l = jnp.sum(functools.reduce(jnp.add, ps), axis=-1, keepdims=True)
    else:
        l = functools.reduce(jnp.add, [jnp.sum(p, axis=-1, keepdims=True) for p in ps])
    o = functools.reduce(jnp.add, [_dot(p.astype(BF16), v) for p, v in zip(ps, values)])
    return o * (1.0 / l)


def _prompt_attn_kernel(pq_ref, q_ref, *rest):
    k_refs = rest[:ATTN_PIECES]
    v_refs = rest[ATTN_PIECES:2 * ATTN_PIECES]
    o_ref, bias_scr = rest[2 * ATTN_PIECES:]
    t = pl.program_id(1)
    qk_scale = HEAD_DIM ** -0.5 * LOG2E

    @pl.when((pl.program_id(0) == 0) & (t == 0))
    def _():
        _build_band_bias(pq_ref, bias_scr, mask_band=True)

    def attend(first_piece):
        for h in range(N_HEADS):
            sl = slice(h * HEAD_DIM, (h + 1) * HEAD_DIM)
            q = q_ref[0, :, sl]
            scores = [_dot_nt(q, k_refs[w][0, :, sl]) * qk_scale
                      + bias_scr[h, :, w * KEY_PIECE:(w + 1) * KEY_PIECE]
                      for w in range(first_piece, ATTN_PIECES)]
            o = _softmax_pv(scores, [v_refs[w][0, :, sl] for w in range(first_piece, ATTN_PIECES)])
            o_ref[0, :, sl] = o.astype(o_ref.dtype)

    for first_piece in range(ATTN_PIECES):
        missing = ATTN_PIECES - 1 - first_piece
        cond = (t == missing) if first_piece > 0 else (t >= missing)
        pl.when(cond)(functools.partial(attend, first_piece))


def _prompt_attn(proj, pq, *, batch, seq):
    tq = KEY_PIECE
    np_ = ATTN_PIECES

    def kv_spec(w, col):
        return pl.BlockSpec((1, tq, ATTN_WIDTH),
                            lambda b, t: (b, jnp.maximum(t - (np_ - 1) + w, 0), col))

    blk = ((1, tq, ATTN_WIDTH), BF16)
    bias_shape = (N_HEADS, tq, np_ * KEY_PIECE)
    return pl.pallas_call(
        _prompt_attn_kernel,
        grid=(batch, seq // tq),
        in_specs=[pl.BlockSpec(pq.shape, lambda b, t: (0, 0, 0)),
                  pl.BlockSpec((1, tq, ATTN_WIDTH), lambda b, t: (b, t, 0))]
        + [kv_spec(w, 1) for w in range(np_)]
        + [kv_spec(w, 2) for w in range(np_)],
        out_specs=pl.BlockSpec((1, tq, ATTN_WIDTH), lambda b, t: (b, t, 0)),
        out_shape=jax.ShapeDtypeStruct((batch, seq, ATTN_WIDTH), BF16),
        scratch_shapes=[pltpu.VMEM(bias_shape, F32)],
        compiler_params=pltpu.CompilerParams(
            dimension_semantics=("arbitrary", "arbitrary"),
            vmem_limit_bytes=_vmem_limit([blk] * (2 + 2 * np_), [(bias_shape, F32)],
                                         temps=12 * _nbytes((tq, tq), F32))),
        name="prompt_band_attn",
    )(pq, proj, *([proj] * (2 * np_)))


def _sample_attn_kernel(pq_ref, q_ref, kn_ref, vn_ref, kf_ref, vf_ref, ck_ref, cv_ref,
                        o_ref, sk_ref, sv_ref, bias_scr):
    qk_scale = HEAD_DIM ** -0.5 * LOG2E
    t_new = q_ref.shape[1]
    n_cache = ck_ref.shape[1] // N_HEADS
    keep = (n_cache - t_new) * N_HEADS

    @pl.when(pl.program_id(0) == 0)
    def _():
        _build_band_bias(pq_ref, bias_scr, mask_band=False)

    for h in range(N_HEADS):
        sl = slice(h * HEAD_DIM, (h + 1) * HEAD_DIM)
        head_rows = pl.ds(h, n_cache, stride=N_HEADS)
        q = q_ref[0, :, sl]
        s_c = _dot_nt(q, ck_ref[0, head_rows, :].astype(BF16)) * qk_scale + bias_scr[h, :, :n_cache]
        s_n = _dot_nt(q, kn_ref[0, :, sl]) * qk_scale + bias_scr[h, :, n_cache:n_cache + t_new]
        o = _softmax_pv([s_c, s_n], [cv_ref[0, head_rows, :].astype(BF16), vn_ref[0, :, sl]])
        o_ref[0, :, sl] = o.astype(o_ref.dtype)
        new_rows = pl.ds(keep + h, t_new, stride=N_HEADS)
        sk_ref[0, new_rows, :] = kf_ref[0, :, sl]
        sv_ref[0, new_rows, :] = vf_ref[0, :, sl]
    sk_ref[0, :keep, :] = ck_ref[0, t_new * N_HEADS:, :]
    sv_ref[0, :keep, :] = cv_ref[0, t_new * N_HEADS:, :]


def _sample_attn(proj, kv_f32, cache_k, cache_v, pq):
    b, t, _ = proj.shape
    lh = cache_k.shape[1]
    assert lh == BAND * N_HEADS and t <= KEY_PIECE
    new_bf = ((1, t, ATTN_WIDTH), BF16)
    new_f = ((1, t, ATTN_WIDTH), F32)
    cache_blk = ((1, lh, HEAD_DIM), F32)
    bias_shape = (N_HEADS, t, ATTN_PIECES * KEY_PIECE)

    def new_spec(col):
        return pl.BlockSpec((1, t, ATTN_WIDTH), lambda i: (i, 0, col))

    cache_spec = pl.BlockSpec((1, lh, HEAD_DIM), lambda i: (i, 0, 0))
    return pl.pallas_call(
        _sample_attn_kernel,
        grid=(b,),
        in_specs=[pl.BlockSpec(pq.shape, lambda i: (0, 0, 0)),
                  new_spec(0), new_spec(1), new_spec(2), new_spec(0), new_spec(1),
                  cache_spec, cache_spec],
        out_specs=[new_spec(0), cache_spec, cache_spec],
        out_shape=[jax.ShapeDtypeStruct((b, t, ATTN_WIDTH), BF16),
                   jax.ShapeDtypeStruct((b, lh, HEAD_DIM), F32),
                   jax.ShapeDtypeStruct((b, lh, HEAD_DIM), F32)],
        scratch_shapes=[pltpu.VMEM(bias_shape, F32)],
        compiler_params=pltpu.CompilerParams(
            dimension_semantics=("arbitrary",),
            vmem_limit_bytes=_vmem_limit(
                [new_bf] * 4 + [new_f] * 2 + [cache_blk] * 4, [(bias_shape, F32)],
                temps=8 * _nbytes((t, BAND), F32) + 2 * _nbytes((BAND, HEAD_DIM), F32))),
        name="sample_band_attn",
    )(pq, proj, proj, proj, kv_f32, kv_f32, cache_k, cache_v)


def _causal_dwconv3(x, prev, w_ref, nseg, lseg):
    c = x.shape[1]
    pos = lax.broadcasted_iota(jnp.int32, (nseg, lseg, c), 1)
    p0 = prev[:, 0:1, :]
    p1 = prev[:, 1:2, :]
    x3 = x.reshape(nseg, lseg, c)
    sh1 = pltpu.roll(x, 1, 0).reshape(nseg, lseg, c)
    sh2 = pltpu.roll(x, 2, 0).reshape(nseg, lseg, c)
    sh1 = jnp.where(pos == 0, p1, sh1)
    sh2 = jnp.where(pos == 0, p0, jnp.where(pos == 1, p1, sh2))
    y = sh2 * w_ref[0:1, :] + sh1 * w_ref[1:2, :] + x3 * w_ref[2:3, :]
    return y.reshape(nseg * lseg, c), x3[:, lseg - (CONV_K - 1):, :]


def _mid_kernel(a_ref, b_ref, c_ref, u_ref, ch_ref, uh_ref, prev_ref, x_ref, mk_ref, mv_ref,
                wconv_ref, ga_ref, gc_ref, wout_ref, g2_ref, wxq_ref, gxq_ref, wxo_ref,
                h_ref, newc_ref, mixed_scr, o_scr, *, nseg, lseg, tiles_per_seq):
    i = pl.program_id(0)

    mixed_scr[:, :ATTN_WIDTH] = _rms(a_ref[...].astype(F32), ga_ref[...]).astype(BF16)

    cu = c_ref[...].astype(F32) * u_ref[...].astype(F32)
    if tiles_per_seq > 1:
        halo = (ch_ref[BF16_SUBLANES - 2:, :].astype(F32)
                * uh_ref[BF16_SUBLANES - 2:, :].astype(F32))
        prev = jnp.where(i % tiles_per_seq == 0, prev_ref[...], halo[None])
    else:
        prev = prev_ref[...]
    conv, new_conv = _causal_dwconv3(cu, prev, wconv_ref, nseg, lseg)
    newc_ref[...] = new_conv
    mixed_scr[:, ATTN_WIDTH:] = _rms(b_ref[...].astype(F32) * conv, gc_ref[...]).astype(BF16)

    h = x_ref[...] + _dot(mixed_scr[...], wout_ref[...])

    qx = _dot(_rms(h, g2_ref[...]).astype(BF16), wxq_ref[...])
    sm_scale = X_HEAD_DIM ** -0.5 * LOG2E
    for hd in range(X_HEADS):
        sl = slice(hd * X_HEAD_DIM, (hd + 1) * X_HEAD_DIM)
        qh = _rms(qx[:, sl], gxq_ref[...]).astype(BF16)
        for s in range(nseg):
            rows = slice(s * lseg, (s + 1) * lseg)
            sc = _dot_nt(qh[rows], mk_ref[s, :, sl].astype(BF16)) * sm_scale
            o = _softmax_pv([sc], [mv_ref[s, :, sl].astype(BF16)])
            o_scr[rows, sl] = o.astype(BF16)

    h_ref[...] = h + _dot(o_scr[...], wxo_ref[...])


def _mid(a, proj, conv_prev, x, mem_k, mem_v, w_conv, g_a, g_c, w_out, g2, w_xq, g_xq, w_xo,
         *, tm, seq_len):
    t, d = x.shape
    cw = conv_prev.shape[2]
    n_mem = mem_k.shape[1]
    lseg = min(tm, seq_len)
    nseg = tm // lseg
    tps = max(1, seq_len // tm)
    halo_rows = BF16_SUBLANES
    halo_per_tile = tm // halo_rows

    def col_spec(col):
        return pl.BlockSpec((tm, cw), lambda i: (i, col))

    def halo_spec(col):
        return pl.BlockSpec((halo_rows, cw),
                            lambda i: (jnp.maximum(i * halo_per_tile - 1, 0), col))

    def const_spec(arr):
        return pl.BlockSpec(arr.shape, lambda i: (0,) * arr.ndim, pipeline_mode=pl.Buffered(1))

    seq_spec3 = lambda rows, width: pl.BlockSpec((nseg, rows, width), lambda i: (i // tps, 0, 0))
    kern = functools.partial(_mid_kernel, nseg=nseg, lseg=lseg, tiles_per_seq=tps)
    pipelined = ([((tm, cw), BF16)] * 4 + [((halo_rows, cw), BF16)] * 2
                 + [((nseg, 2, cw), F32)] * 2 + [((tm, d), F32)] * 2
                 + [((nseg, n_mem, X_WIDTH), F32)] * 2)
    resident = [(w.shape, w.dtype) for w in (w_out, w_xq, w_xo)]
    resident += [((tm, d), BF16), ((tm, X_WIDTH), BF16)]
    return pl.pallas_call(
        kern,
        grid=(t // tm,),
        in_specs=[
            pl.BlockSpec((tm, ATTN_WIDTH), lambda i: (i, 0)),
            col_spec(3), col_spec(4), col_spec(5), halo_spec(4), halo_spec(5),
            seq_spec3(2, cw),
            pl.BlockSpec((tm, d), lambda i: (i, 0)),
            seq_spec3(n_mem, X_WIDTH), seq_spec3(n_mem, X_WIDTH),
            const_spec(w_conv), const_spec(g_a), const_spec(g_c), const_spec(w_out),
            const_spec(g2), const_spec(w_xq), const_spec(g_xq), const_spec(w_xo),
        ],
        out_specs=[pl.BlockSpec((tm, d), lambda i: (i, 0)), seq_spec3(2, cw)],
        out_shape=[jax.ShapeDtypeStruct((t, d), F32),
                   jax.ShapeDtypeStruct(conv_prev.shape, F32)],
        scratch_shapes=[pltpu.VMEM((tm, d), BF16), pltpu.VMEM((tm, X_WIDTH), BF16)],
        compiler_params=pltpu.CompilerParams(
            dimension_semantics=("arbitrary",),
            vmem_limit_bytes=_vmem_limit(pipelined, resident, temps=8 * _nbytes((tm, d), F32))),
        name="mid",
    )(a, proj, proj, proj, proj, proj, conv_prev, x, mem_k, mem_v,
      w_conv, g_a, g_c, w_out, g2, w_xq, g_xq, w_xo)


def _ffn_kernel(h_ref, g3_ref, wup_ref, wgate_ref, wconv_ref, wdown_ref, prev_ref,
                y_ref, newf_ref, n3_scr, carry_scr, *, nseg, lseg, tiles_per_seq):
    i = pl.program_id(0)
    f = pl.program_id(1)
    tm = h_ref.shape[0]

    @pl.when(f == 0)
    def _():
        h = h_ref[...]
        n3_scr[...] = _rms(h, g3_ref[...]).astype(BF16)
        y_ref[...] = h

    n3 = n3_scr[...]
    up = _dot(n3, wup_ref[...])
    gate = _dot(n3, wgate_ref[...])
    if tiles_per_seq > 1:
        prev = jnp.where(i % tiles_per_seq == 0, prev_ref[...], carry_scr[f][None])
        carry_scr[f] = up[tm - (CONV_K - 1):, :]
    else:
        prev = prev_ref[...]
    conv, new_ffn = _causal_dwconv3(up, prev, wconv_ref, nseg, lseg)
    newf_ref[f] = new_ffn
    act = conv / (1.0 + jnp.exp(-conv)) * gate
    y_ref[...] += _dot(act.astype(BF16), wdown_ref[...])


def _ffn(h, g3, w_up, w_gate, w_conv, w_down, ffn_prev, *, tm, tf, seq_len):
    t, d = h.shape
    dff = w_up.shape[1]
    n_seq = ffn_prev.shape[0]
    lseg = min(tm, seq_len)
    nseg = tm // lseg
    tps = max(1, seq_len // tm)
    nf = dff // tf
    kern = functools.partial(_ffn_kernel, nseg=nseg, lseg=lseg, tiles_per_seq=tps)
    prev_spec = pl.BlockSpec((nseg, CONV_K - 1, tf), lambda i, f: (i // tps, 0, f))
    newf_blk = (nf, nseg, CONV_K - 1, tf)
    pipelined = ([((tm, d), F32)] * 2 + [((d, tf), BF16)] * 3 + [((nseg, 2, tf), F32)]
                 + [(newf_blk, F32), ((CONV_K, tf), F32)])
    resident = [((tm, d), BF16), ((nf, CONV_K - 1, tf), F32)]
    y, new_ffn = pl.pallas_call(
        kern,
        grid=(t // tm, nf),
        in_specs=[
            pl.BlockSpec((tm, d), lambda i, f: (i, 0)),
            pl.BlockSpec((1, d), lambda i, f: (0, 0)),
            pl.BlockSpec((d, tf), lambda i, f: (0, f)),
            pl.BlockSpec((d, tf), lambda i, f: (0, f)),
            pl.BlockSpec((CONV_K, tf), lambda i, f: (0, f)),
            pl.BlockSpec((tf, d), lambda i, f: (f, 0)),
            prev_spec,
        ],
        out_specs=[pl.BlockSpec((tm, d), lambda i, f: (i, 0)),
                   pl.BlockSpec(newf_blk, lambda i, f: (0, i // tps, 0, 0))],
        out_shape=[jax.ShapeDtypeStruct((t, d), F32),
                   jax.ShapeDtypeStruct((nf, n_seq, CONV_K - 1, tf), F32)],
        scratch_shapes=[pltpu.VMEM((tm, d), BF16), pltpu.VMEM((nf, CONV_K - 1, tf), F32)],
        compiler_params=pltpu.CompilerParams(
            dimension_semantics=("arbitrary", "arbitrary"),
            vmem_limit_bytes=_vmem_limit(pipelined, resident, temps=10 * _nbytes((tm, tf), F32))),
        name="ffn",
    )(h, g3, w_up, w_gate, w_conv, w_down, ffn_prev)
    return y, new_ffn.transpose(1, 2, 0, 3).reshape(n_seq, CONV_K - 1, dff)


def _layer(x, *, seq_len, attn_fn, conv_prev, ffn_prev, mem_k, mem_v, w, tm_proj, tm_mid, tm_ffn):
    proj, kv_f32 = _proj(x, w["g_norm1"], w["w_in"], w["g_in_cols"], tm=tm_proj, tn=512,
                         head_dim=HEAD_DIM, norm_cols=2 * ATTN_WIDTH, out_dtype=BF16,
                         aux_cols=(ATTN_WIDTH, 3 * ATTN_WIDTH), name="in_proj")
    a, new_k, new_v = attn_fn(proj, kv_f32)
    h, new_conv = _mid(a.reshape(-1, ATTN_WIDTH), proj, conv_prev, x, mem_k, mem_v,
                       w["w_conv_mix"], w["g_out_attn"], w["g_out_conv"], w["w_out"],
                       w["g_norm2"], w["w_xq"], w["g_xq"], w["w_xo"], tm=tm_mid, seq_len=seq_len)
    y, new_ffn = _ffn(h, w["g_norm3"], w["w_up"], w["w_gate"], w["w_ffn_conv"], w["w_down"],
                      ffn_prev, tm=tm_ffn, tf=512, seq_len=seq_len)
    return y, new_k, new_v, new_conv, new_ffn


def kernel(x_prompt, x_sample, mem_prompt, cache_attn_k, cache_attn_v, cache_conv, cache_ffn_conv, cache_mem_k, cache_mem_v, g_norm1, w_in, g_q, g_k, rel_bias, w_conv_mix, g_out_attn, g_out_conv, w_out, g_norm2, g_mem_norm, w_xq, w_xkv, g_xq, g_xk, w_xo, g_norm3, w_up, w_gate, w_ffn_conv, w_down):
    bp, sp, d = x_prompt.shape
    bs, ss, _ = x_sample.shape
    depth = w_in.shape[0]
    in_width = w_in.shape[2]
    conv_ch = w_conv_mix.shape[2]
    d_ff = w_up.shape[2]
    n_mem = mem_prompt.shape[1]
    n_cache = cache_attn_k.shape[2]
    keep = min(BAND, sp)

    yp = x_prompt.reshape(bp * sp, d)
    ys = x_sample.reshape(bs * ss, d)
    outs = [[] for _ in range(10)]
    for l in range(depth):
        row = lambda g: g[l][None, :]
        w = dict(
            g_norm1=row(g_norm1), w_in=w_in[l].astype(BF16),
            g_in_cols=jnp.concatenate([jnp.tile(g_q[l], N_HEADS), jnp.tile(g_k[l], N_HEADS),
                                       jnp.ones((in_width - 2 * ATTN_WIDTH,), F32)])[None, :],
            w_conv_mix=w_conv_mix[l], g_out_attn=row(g_out_attn), g_out_conv=row(g_out_conv),
            w_out=w_out[l].astype(BF16), g_norm2=row(g_norm2), w_xq=w_xq[l].astype(BF16),
            g_xq=row(g_xq), w_xo=w_xo[l].astype(BF16), g_norm3=row(g_norm3),
            w_up=w_up[l].astype(BF16), w_gate=w_gate[l].astype(BF16),
            w_ffn_conv=w_ffn_conv[l], w_down=w_down[l].astype(BF16))
        pq = _rel_rows(rel_bias[l])

        g_mem_cols = jnp.concatenate([jnp.tile(g_xk[l], X_HEADS), jnp.ones((X_WIDTH,), F32)])[None, :]
        (mem_kv,) = _proj(mem_prompt.reshape(bp * n_mem, d), row(g_mem_norm), w_xkv[l].astype(BF16),
                          g_mem_cols, tm=bp * n_mem, tn=512, head_dim=X_HEAD_DIM,
                          norm_cols=X_WIDTH, out_dtype=F32, name="mem_kv")
        mem_kv = mem_kv.reshape(bp, n_mem, 2 * X_WIDTH)
        mk_p, mv_p = mem_kv[:, :, :X_WIDTH], mem_kv[:, :, X_WIDTH:]

        def prompt_attn(proj, kv_f32):
            a = _prompt_attn(proj.reshape(bp, sp, in_width), pq, batch=bp, seq=sp)
            kv = kv_f32.reshape(bp, sp, 2 * ATTN_WIDTH)[:, sp - keep:]
            return a, kv[:, :, :ATTN_WIDTH], kv[:, :, ATTN_WIDTH:]

        yp, pk, pv, pc, pf = _layer(
            yp, seq_len=sp, attn_fn=prompt_attn,
            conv_prev=jnp.zeros((bp, CONV_K - 1, conv_ch), F32),
            ffn_prev=jnp.zeros((bp, CONV_K - 1, d_ff), F32),
            mem_k=mk_p, mem_v=mv_p, w=w, tm_proj=1024, tm_mid=256, tm_ffn=512)

        ck = cache_attn_k[l].reshape(bs, n_cache * N_HEADS, HEAD_DIM)
        cv = cache_attn_v[l].reshape(bs, n_cache * N_HEADS, HEAD_DIM)

        def sample_attn(proj, kv_f32):
            return _sample_attn(proj.reshape(bs, ss, in_width),
                                kv_f32.reshape(bs, ss, 2 * ATTN_WIDTH), ck, cv, pq)

        ys, sk, sv, sc, sf = _layer(
            ys, seq_len=ss, attn_fn=sample_attn, conv_prev=cache_conv[l], ffn_prev=cache_ffn_conv[l],
            mem_k=cache_mem_k[l].reshape(bs, n_mem, X_WIDTH),
            mem_v=cache_mem_v[l].reshape(bs, n_mem, X_WIDTH),
            w=w, tm_proj=1024, tm_mid=256, tm_ffn=512)

        for lst, val in zip(outs, (
                pk.reshape(bp, keep, N_HEADS, HEAD_DIM), pv.reshape(bp, keep, N_HEADS, HEAD_DIM),
                pc, pf,
                mk_p.reshape(bp, n_mem, X_HEADS, X_HEAD_DIM), mv_p.reshape(bp, n_mem, X_HEADS, X_HEAD_DIM),
                sk.reshape(bs, n_cache, N_HEADS, HEAD_DIM), sv.reshape(bs, n_cache, N_HEADS, HEAD_DIM),
                sc, sf)):
            lst.append(val)

    return (yp.reshape(bp, sp, d), ys.reshape(bs, ss, d)) + tuple(jnp.stack(o) for o in outs)
```

```python
import functools
import math

import jax
import jax.numpy as jnp
import numpy as np
from jax import lax
from jax.experimental import pallas as pl
from jax.experimental.pallas import tpu as pltpu

CHUNK = 64
N_PREV_CHUNKS = 8
BAND = N_PREV_CHUNKS * CHUNK
N_HEADS = 8
HEAD_DIM = 128
ATTN_WIDTH = N_HEADS * HEAD_DIM
REL_CLIP = 128
CONV_K = 3
X_HEADS = 4
X_HEAD_DIM = 256
X_WIDTH = X_HEADS * X_HEAD_DIM
EPS = 1e-6
NEG_INF = -1e30
LOG2E = math.log2(math.e)

BF16 = jnp.bfloat16
F32 = jnp.float32

V7X_VMEM_BYTES = 64 * 1024 * 1024

KEY_PIECE = 2 * REL_CLIP
ATTN_PIECES = BAND // KEY_PIECE + 1
assert KEY_PIECE % CHUNK == 0 and BAND % KEY_PIECE == 0


def _nbytes(shape, dtype):
    return int(np.prod(shape)) * jnp.dtype(dtype).itemsize


def _vmem_limit(pipelined, resident=(), temps=0):
    est = 2 * sum(_nbytes(s, d) for s, d in pipelined)
    est += sum(_nbytes(s, d) for s, d in resident) + temps
    return min(int(est * 1.25) + (2 << 20), V7X_VMEM_BYTES - (6 << 20))


def _rms(xf, g):
    return xf * lax.rsqrt(jnp.mean(xf * xf, axis=-1, keepdims=True) + EPS) * g


def _dot(a, b):
    return jnp.dot(a, b, preferred_element_type=F32)


def _dot_nt(a, b):
    return lax.dot_general(a, b, (((1,), (1,)), ((), ())), preferred_element_type=F32)


def _lookup(j, table):
    return functools.reduce(lambda acc, kv: jnp.where(j == kv[0], kv[1], acc),
                            enumerate(table), jnp.int32(0))


def _proj_kernel(x_ref, gin_ref, w_ref, gcol_ref, *rest, head_dim, norm_flags, has_aux,
                 row_chunk):
    if has_aux:
        out_ref, aux_ref, n_scr = rest
    else:
        out_ref, n_scr = rest
    j = pl.program_id(1)
    tn = out_ref.shape[1]

    @pl.when(j == 0)
    def _():
        n_scr[...] = _rms(x_ref[...], gin_ref[...]).astype(BF16)

    is_norm = _lookup(j, norm_flags) != 0
    for c in range(n_scr.shape[0] // row_chunk):
        rows = slice(c * row_chunk, (c + 1) * row_chunk)
        acc = _dot(n_scr[rows, :], w_ref[...])
        for h in range(tn // head_dim):
            sl = slice(h * head_dim, (h + 1) * head_dim)
            a = acc[:, sl]
            r = lax.rsqrt(jnp.mean(a * a, axis=-1, keepdims=True) + EPS)
            y = a * jnp.where(is_norm, r, 1.0) * gcol_ref[:, sl]
            out_ref[rows, sl] = y.astype(out_ref.dtype)
            if has_aux:
                aux_ref[rows, sl] = y


def _proj(x, g_in, w, gcol, *, tm, tn, head_dim, norm_cols, out_dtype, aux_cols=None, name,
          row_chunk=256):
    t, d = x.shape
    n = w.shape[1]
    nj = n // tn
    has_aux = aux_cols is not None
    aux_tiles = tuple(range(aux_cols[0] // tn, aux_cols[1] // tn)) if has_aux else ()
    order = tuple(c for c in range(nj) if c not in aux_tiles) + aux_tiles
    norm_flags = tuple(int(c < norm_cols // tn) for c in order)
    n_aux = len(aux_tiles)
    col = lambda j: _lookup(j, order)
    out_shape = [jax.ShapeDtypeStruct((t, n), out_dtype)]
    out_specs = [pl.BlockSpec((tm, tn), lambda i, j: (i, col(j)))]
    pipelined = [((tm, d), F32), ((d, tn), BF16), ((tm, tn), out_dtype)]
    if has_aux:
        out_shape.append(jax.ShapeDtypeStruct((t, n_aux * tn), F32))
        out_specs.append(pl.BlockSpec(
            (tm, tn), lambda i, j: (i, jnp.maximum(j - (nj - n_aux), 0))))
        pipelined.append(((tm, tn), F32))
    kern = functools.partial(_proj_kernel, head_dim=head_dim, norm_flags=norm_flags,
                             has_aux=has_aux, row_chunk=min(row_chunk, tm))
    return pl.pallas_call(
        kern,
        grid=(t // tm, nj),
        in_specs=[
            pl.BlockSpec((tm, d), lambda i, j: (i, 0)),
            pl.BlockSpec((1, d), lambda i, j: (0, 0)),
            pl.BlockSpec((d, tn), lambda i, j: (0, col(j))),
            pl.BlockSpec((1, tn), lambda i, j: (0, col(j))),
        ],
        out_specs=out_specs,
        out_shape=out_shape,
        scratch_shapes=[pltpu.VMEM((tm, d), BF16)],
        compiler_params=pltpu.CompilerParams(
            dimension_semantics=("arbitrary", "arbitrary"),
            vmem_limit_bytes=_vmem_limit(pipelined, [((tm, d), BF16)],
                                         temps=3 * _nbytes((tm, tn), F32) + _nbytes((tm, d), F32))),
        name=name,
    )(x, g_in, w, gcol)


def _toeplitz(vec, rows):
    x = jnp.broadcast_to(vec, (rows, KEY_PIECE))
    r = lax.broadcasted_iota(jnp.int32, (rows, KEY_PIECE), 0)
    for b in range((rows - 1).bit_length()):
        x = jnp.where((r >> b) & 1 == 1, pltpu.roll(x, 1 << b, 1), x)
    return x


def _build_band_bias(pq_ref, bias_scr, mask_band):
    rows = bias_scr.shape[1]
    r = lax.broadcasted_iota(jnp.int32, (rows, KEY_PIECE), 0)
    c = lax.broadcasted_iota(jnp.int32, (rows, KEY_PIECE), 1)
    upper = c >= r
    for h in range(N_HEADS):
        top = jnp.broadcast_to(pq_ref[h, 0:1, :], (rows, KEY_PIECE))
        tp = _toeplitz(pq_ref[h, 1:2, :], rows)
        tq = _toeplitz(pq_ref[h, 2:3, :], rows)
        pieces = [top] * (ATTN_PIECES - 2) + [jnp.where(upper, tp, top), jnp.where(upper, tq, tp)]
        for w, piece in enumerate(pieces):
            piece = piece * LOG2E
            if mask_band:
                kc = w * (KEY_PIECE // CHUNK) + c // CHUNK
                qc = r // CHUNK
                piece = jnp.where((kc >= qc) & (kc <= qc + N_PREV_CHUNKS), piece, NEG_INF)
            bias_scr[h, :, w * KEY_PIECE:(w + 1) * KEY_PIECE] = piece


def _rel_rows(table):
    h = table.shape[0]
    top = table[:, 2 * REL_CLIP:]
    lo = table[:, :1]
    p = jnp.concatenate([jnp.broadcast_to(top, (h, REL_CLIP + 1)),
                         table[:, 2 * REL_CLIP - 1:REL_CLIP:-1]], axis=1)
    q = jnp.concatenate([table[:, REL_CLIP::-1],
                         jnp.broadcast_to(lo, (h, KEY_PIECE - REL_CLIP - 1))], axis=1)
    return jnp.stack([jnp.broadcast_to(top, (h, KEY_PIECE)), p, q], axis=1).astype(F32)


def _softmax_pv(scores, values):
    widths = {s.shape[1] for s in scores}
    if len(widths) == 1:
        m = jnp.max(functools.reduce(jnp.maximum, scores), axis=-1, keepdims=True)
    else:
        m = functools.reduce(jnp.maximum, [jnp.max(s, axis=-1, keepdims=True) for s in scores])
    ps = [jnp.exp2(s - m) for s in scores]
    if len(widths) == 1:
        l = jnp.sum(functools.reduce(jnp.add, ps), axis=-1, keepdims=True)
    else:
        l = functools.reduce(jnp.add, [jnp.sum(p, axis=-1, keepdims=True) for p in ps])
    o = functools.reduce(jnp.add, [_dot(p.astype(BF16), v) for p, v in zip(ps, values)])
    return o * (1.0 / l)


def _prompt_attn_kernel(pq_ref, q_ref, *rest):
    k_refs = rest[:ATTN_PIECES]
    v_refs = rest[ATTN_PIECES:2 * ATTN_PIECES]
    o_ref, bias_scr = rest[2 * ATTN_PIECES:]
    t = pl.program_id(1)
    qk_scale = HEAD_DIM ** -0.5 * LOG2E

    @pl.when((pl.program_id(0) == 0) & (t == 0))
    def _():
        _build_band_bias(pq_ref, bias_scr, mask_band=True)

    def attend(first_piece):
        for h in range(N_HEADS):
            sl = slice(h * HEAD_DIM, (h + 1) * HEAD_DIM)
            q = q_ref[0, :, sl]
            scores = [_dot_nt(q, k_refs[w][0, :, sl]) * qk_scale
                      + bias_scr[h, :, w * KEY_PIECE:(w + 1) * KEY_PIECE]
                      for w in range(first_piece, ATTN_PIECES)]
            o = _softmax_pv(scores, [v_refs[w][0, :, sl] for w in range(first_piece, ATTN_PIECES)])
            o_ref[0, :, sl] = o.astype(o_ref.dtype)

    for first_piece in range(ATTN_PIECES):
        missing = ATTN_PIECES - 1 - first_piece
        cond = (t == missing) if first_piece > 0 else (t >= missing)
        pl.when(cond)(functools.partial(attend, first_piece))


def _prompt_attn(proj, pq, *, batch, seq):
    tq = KEY_PIECE
    np_ = ATTN_PIECES

    def kv_spec(w, col):
        return pl.BlockSpec((1, tq, ATTN_WIDTH),
                            lambda b, t: (b, jnp.maximum(t - (np_ - 1) + w, 0), col))

    blk = ((1, tq, ATTN_WIDTH), BF16)
    bias_shape = (N_HEADS, tq, np_ * KEY_PIECE)
    return pl.pallas_call(
        _prompt_attn_kernel,
        grid=(batch, seq // tq),
        in_specs=[pl.BlockSpec(pq.shape, lambda b, t: (0, 0, 0)),
                  pl.BlockSpec((1, tq, ATTN_WIDTH), lambda b, t: (b, t, 0))]
        + [kv_spec(w, 1) for w in range(np_)]
        + [kv_spec(w, 2) for w in range(np_)],
        out_specs=pl.BlockSpec((1, tq, ATTN_WIDTH), lambda b, t: (b, t, 0)),
        out_shape=jax.ShapeDtypeStruct((batch, seq, ATTN_WIDTH), BF16),
        scratch_shapes=[pltpu.VMEM(bias_shape, F32)],
        compiler_params=pltpu.CompilerParams(
            dimension_semantics=("arbitrary", "arbitrary"),
            vmem_limit_bytes=_vmem_limit([blk] * (2 + 2 * np_), [(bias_shape, F32)],
                                         temps=12 * _nbytes((tq, tq), F32))),
        name="prompt_band_attn",
    )(pq, proj, *([proj] * (2 * np_)))


def _sample_attn_kernel(pq_ref, q_ref, kn_ref, vn_ref, kf_ref, vf_ref, ck_ref, cv_ref,
                        o_ref, sk_ref, sv_ref, bias_scr):
    qk_scale = HEAD_DIM ** -0.5 * LOG2E
    t_new = q_ref.shape[1]
    n_cache = ck_ref.shape[1] // N_HEADS
    keep = (n_cache - t_new) * N_HEADS

    @pl.when(pl.program_id(0) == 0)
    def _():
        _build_band_bias(pq_ref, bias_scr, mask_band=False)

    for h in range(N_HEADS):
        sl = slice(h * HEAD_DIM, (h + 1) * HEAD_DIM)
        head_rows = pl.ds(h, n_cache, stride=N_HEADS)
        q = q_ref[0, :, sl]
        s_c = _dot_nt(q, ck_ref[0, head_rows, :].astype(BF16)) * qk_scale + bias_scr[h, :, :n_cache]
        s_n = _dot_nt(q, kn_ref[0, :, sl]) * qk_scale + bias_scr[h, :, n_cache:n_cache + t_new]
        o = _softmax_pv([s_c, s_n], [cv_ref[0, head_rows, :].astype(BF16), vn_ref[0, :, sl]])
        o_ref[0, :, sl] = o.astype(o_ref.dtype)
        new_rows = pl.ds(keep + h, t_new, stride=N_HEADS)
        sk_ref[0, new_rows, :] = kf_ref[0, :, sl]
        sv_ref[0, new_rows, :] = vf_ref[0, :, sl]
    sk_ref[0, :keep, :] = ck_ref[0, t_new * N_HEADS:, :]
    sv_ref[0, :keep, :] = cv_ref[0, t_new * N_HEADS:, :]


def _sample_attn(proj, kv_f32, cache_k, cache_v, pq):
    b, t, _ = proj.shape
    lh = cache_k.shape[1]
    assert lh == BAND * N_HEADS and t <= KEY_PIECE
    new_bf = ((1, t, ATTN_WIDTH), BF16)
    new_f = ((1, t, ATTN_WIDTH), F32)
    cache_blk = ((1, lh, HEAD_DIM), F32)
    bias_shape = (N_HEADS, t, ATTN_PIECES * KEY_PIECE)

    def new_spec(col):
        return pl.BlockSpec((1, t, ATTN_WIDTH), lambda i: (i, 0, col))

    cache_spec = pl.BlockSpec((1, lh, HEAD_DIM), lambda i: (i, 0, 0))
    return pl.pallas_call(
        _sample_attn_kernel,
        grid=(b,),
        in_specs=[pl.BlockSpec(pq.shape, lambda i: (0, 0, 0)),
                  new_spec(0), new_spec(1), new_spec(2), new_spec(0), new_spec(1),
                  cache_spec, cache_spec],
        out_specs=[new_spec(0), cache_spec, cache_spec],
        out_shape=[jax.ShapeDtypeStruct((b, t, ATTN_WIDTH), BF16),
                   jax.ShapeDtypeStruct((b, lh, HEAD_DIM), F32),
                   jax.ShapeDtypeStruct((b, lh, HEAD_DIM), F32)],
        scratch_shapes=[pltpu.VMEM(bias_shape, F32)],
        compiler_params=pltpu.CompilerParams(
            dimension_semantics=("arbitrary",),
            vmem_limit_bytes=_vmem_limit(
                [new_bf] * 4 + [new_f] * 2 + [cache_blk] * 4, [(bias_shape, F32)],
                temps=8 * _nbytes((t, BAND), F32) + 2 * _nbytes((BAND, HEAD_DIM), F32))),
        name="sample_band_attn",
    )(pq, proj, proj, proj, kv_f32, kv_f32, cache_k, cache_v)


def _causal_dwconv3(x, prev, w_ref, nseg, lseg):
    c = x.shape[1]
    pos = lax.broadcasted_iota(jnp.int32, (nseg, lseg, c), 1)
    p0 = prev[:, 0:1, :]
    p1 = prev[:, 1:2, :]
    x3 = x.reshape(nseg, lseg, c)
    sh1 = pltpu.roll(x, 1, 0).reshape(nseg, lseg, c)
    sh2 = pltpu.roll(x, 2, 0).reshape(nseg, lseg, c)
    sh1 = jnp.where(pos == 0, p1, sh1)
    sh2 = jnp.where(pos == 0, p0, jnp.where(pos == 1, p1, sh2))
    y = sh2 * w_ref[0:1, :] + sh1 * w_ref[1:2, :] + x3 * w_ref[2:3, :]
    return y.reshape(nseg * lseg, c), x3[:, lseg - (CONV_K - 1):, :]


def _conv_prev(c, row_chunk, lseg, tiles_per_seq, tile_idx, prev_ref, carry, tail):
    if (c * row_chunk) % lseg != 0:
        return tail[None]
    if tiles_per_seq > 1:
        return jnp.where(tile_idx % tiles_per_seq == 0, prev_ref[...], carry[None])
    s0 = c * row_chunk // lseg
    return prev_ref[s0:s0 + max(1, row_chunk // lseg)]


def _mid_kernel(a_ref, b_ref, c_ref, u_ref, prev_ref, x_ref, mk_ref, mv_ref,
                wconv_ref, ga_ref, gc_ref, wout_ref, g2_ref, wxq_ref, gxq_ref, wxo_ref,
                h_ref, newc_ref, mixed_scr, o_scr, carry_scr, *, lseg, tiles_per_seq, row_chunk):
    i = pl.program_id(0)
    tm = x_ref.shape[0]
    seg_per_chunk = max(1, row_chunk // lseg)
    lseg_c = min(row_chunk, lseg)
    sm_scale = X_HEAD_DIM ** -0.5 * LOG2E
    tail = None
    for c in range(tm // row_chunk):
        rows = slice(c * row_chunk, (c + 1) * row_chunk)
        s0 = c * row_chunk // lseg
        mixed_scr[rows, :ATTN_WIDTH] = _rms(a_ref[rows, :].astype(F32), ga_ref[...]).astype(BF16)

        cu = c_ref[rows, :].astype(F32) * u_ref[rows, :].astype(F32)
        prev = _conv_prev(c, row_chunk, lseg, tiles_per_seq, i, prev_ref, carry_scr[...], tail)
        tail = cu[row_chunk - (CONV_K - 1):, :]
        conv, new_conv = _causal_dwconv3(cu, prev, wconv_ref, seg_per_chunk, lseg_c)
        if ((c + 1) * row_chunk) % lseg == 0:
            newc_ref[s0:s0 + seg_per_chunk] = new_conv
        mixed_scr[rows, ATTN_WIDTH:] = _rms(b_ref[rows, :].astype(F32) * conv,
                                            gc_ref[...]).astype(BF16)

        h = x_ref[rows, :] + _dot(mixed_scr[rows, :], wout_ref[...])

        qx = _dot(_rms(h, g2_ref[...]).astype(BF16), wxq_ref[...])
        for hd in range(X_HEADS):
            sl = slice(hd * X_HEAD_DIM, (hd + 1) * X_HEAD_DIM)
            qh = _rms(qx[:, sl], gxq_ref[...]).astype(BF16)
            for s in range(seg_per_chunk):
                srows = slice(s * lseg_c, (s + 1) * lseg_c)
                sc = _dot_nt(qh[srows], mk_ref[s0 + s, :, sl].astype(BF16)) * sm_scale
                o = _softmax_pv([sc], [mv_ref[s0 + s, :, sl].astype(BF16)])
                o_scr[c * row_chunk + s * lseg_c:c * row_chunk + (s + 1) * lseg_c, sl] = o.astype(BF16)

        h_ref[rows, :] = h + _dot(o_scr[rows, :], wxo_ref[...])
    if tiles_per_seq > 1:
        carry_scr[...] = tail


def _mid(a, proj, conv_prev, x, mem_k, mem_v, w_conv, g_a, g_c, w_out, g2, w_xq, g_xq, w_xo,
         *, tm, seq_len, row_chunk=256):
    t, d = x.shape
    cw = conv_prev.shape[2]
    n_mem = mem_k.shape[1]
    lseg = min(tm, seq_len)
    nseg = tm // lseg
    tps = max(1, seq_len // tm)
    row_chunk = min(row_chunk, tm)
    assert lseg % row_chunk == 0 or row_chunk % lseg == 0

    def col_spec(col):
        return pl.BlockSpec((tm, cw), lambda i: (i, col))

    def const_spec(arr):
        return pl.BlockSpec(arr.shape, lambda i: (0,) * arr.ndim, pipeline_mode=pl.Buffered(1))

    seq_spec3 = lambda rows, width: pl.BlockSpec((nseg, rows, width), lambda i: (i // tps, 0, 0))
    kern = functools.partial(_mid_kernel, lseg=lseg, tiles_per_seq=tps, row_chunk=row_chunk)
    pipelined = ([((tm, cw), BF16)] * 4 + [((nseg, 2, cw), F32)] * 2 + [((tm, d), F32)] * 2
                 + [((nseg, n_mem, X_WIDTH), F32)] * 2)
    resident = [(w.shape, w.dtype) for w in (w_out, w_xq, w_xo)]
    resident += [((tm, d), BF16), ((tm, X_WIDTH), BF16)]
    return pl.pallas_call(
        kern,
        grid=(t // tm,),
        in_specs=[
            pl.BlockSpec((tm, ATTN_WIDTH), lambda i: (i, 0)),
            col_spec(3), col_spec(4), col_spec(5),
            seq_spec3(2, cw),
            pl.BlockSpec((tm, d), lambda i: (i, 0)),
            seq_spec3(n_mem, X_WIDTH), seq_spec3(n_mem, X_WIDTH),
            const_spec(w_conv), const_spec(g_a), const_spec(g_c), const_spec(w_out),
            const_spec(g2), const_spec(w_xq), const_spec(g_xq), const_spec(w_xo),
        ],
        out_specs=[pl.BlockSpec((tm, d), lambda i: (i, 0)), seq_spec3(2, cw)],
        out_shape=[jax.ShapeDtypeStruct((t, d), F32),
                   jax.ShapeDtypeStruct(conv_prev.shape, F32)],
        scratch_shapes=[pltpu.VMEM((tm, d), BF16), pltpu.VMEM((tm, X_WIDTH), BF16),
                        pltpu.VMEM((CONV_K - 1, cw), F32)],
        compiler_params=pltpu.CompilerParams(
            dimension_semantics=("arbitrary",),
            vmem_limit_bytes=_vmem_limit(pipelined, resident,
                                         temps=8 * _nbytes((row_chunk, d), F32))),
        name="mid",
    )(a, proj, proj, proj, conv_prev, x, mem_k, mem_v,
      w_conv, g_a, g_c, w_out, g2, w_xq, g_xq, w_xo)


def _ffn_kernel(h_ref, g3_ref, wup_ref, wgate_ref, wconv_ref, wdown_ref, prev_ref,
                y_ref, newf_ref, n3_scr, carry_scr, *, lseg, tiles_per_seq, row_chunk):
    i = pl.program_id(0)
    f = pl.program_id(1)
    tm = h_ref.shape[0]

    @pl.when(f == 0)
    def _():
        h = h_ref[...]
        n3_scr[...] = _rms(h, g3_ref[...]).astype(BF16)
        y_ref[...] = h

    seg_per_chunk = max(1, row_chunk // lseg)
    lseg_c = min(row_chunk, lseg)
    tail = None
    for c in range(tm // row_chunk):
        rows = slice(c * row_chunk, (c + 1) * row_chunk)
        n3 = n3_scr[rows, :]
        up = _dot(n3, wup_ref[...])
        gate = _dot(n3, wgate_ref[...])
        prev = _conv_prev(c, row_chunk, lseg, tiles_per_seq, i, prev_ref, carry_scr[f], tail)
        tail = up[row_chunk - (CONV_K - 1):, :]
        conv, new_ffn = _causal_dwconv3(up, prev, wconv_ref, seg_per_chunk, lseg_c)
        if ((c + 1) * row_chunk) % lseg == 0:
            s0 = c * row_chunk // lseg
            newf_ref[f, s0:s0 + seg_per_chunk] = new_ffn
        act = conv / (1.0 + jnp.exp(-conv)) * gate
        y_ref[rows, :] += _dot(act.astype(BF16), wdown_ref[...])
    if tiles_per_seq > 1:
        carry_scr[f] = tail


def _ffn(h, g3, w_up, w_gate, w_conv, w_down, ffn_prev, *, tm, tf, seq_len, row_chunk=512):
    t, d = h.shape
    dff = w_up.shape[1]
    n_seq = ffn_prev.shape[0]
    lseg = min(tm, seq_len)
    nseg = tm // lseg
    tps = max(1, seq_len // tm)
    nf = dff // tf
    row_chunk = min(row_chunk, tm)
    assert lseg % row_chunk == 0 or row_chunk % lseg == 0
    kern = functools.partial(_ffn_kernel, lseg=lseg, tiles_per_seq=tps, row_chunk=row_chunk)
    prev_spec = pl.BlockSpec((nseg, CONV_K - 1, tf), lambda i, f: (i // tps, 0, f))
    newf_blk = (nf, nseg, CONV_K - 1, tf)
    pipelined = ([((tm, d), F32)] * 2 + [((d, tf), BF16)] * 3 + [((nseg, 2, tf), F32)]
                 + [(newf_blk, F32), ((CONV_K, tf), F32)])
    resident = [((tm, d), BF16), ((nf, CONV_K - 1, tf), F32)]
    y, new_ffn = pl.pallas_call(
        kern,
        grid=(t // tm, nf),
        in_specs=[
            pl.BlockSpec((tm, d), lambda i, f: (i, 0)),
            pl.BlockSpec((1, d), lambda i, f: (0, 0)),
            pl.BlockSpec((d, tf), lambda i, f: (0, f)),
            pl.BlockSpec((d, tf), lambda i, f: (0, f)),
            pl.BlockSpec((CONV_K, tf), lambda i, f: (0, f)),
            pl.BlockSpec((tf, d), lambda i, f: (f, 0)),
            prev_spec,
        ],
        out_specs=[pl.BlockSpec((tm, d), lambda i, f: (i, 0)),
                   pl.BlockSpec(newf_blk, lambda i, f: (0, i // tps, 0, 0))],
        out_shape=[jax.ShapeDtypeStruct((t, d), F32),
                   jax.ShapeDtypeStruct((nf, n_seq, CONV_K - 1, tf), F32)],
        scratch_shapes=[pltpu.VMEM((tm, d), BF16), pltpu.VMEM((nf, CONV_K - 1, tf), F32)],
        compiler_params=pltpu.CompilerParams(
            dimension_semantics=("arbitrary", "arbitrary"),
            vmem_limit_bytes=_vmem_limit(pipelined, resident, temps=10 * _nbytes((tm, tf), F32))),
        name="ffn",
    )(h, g3, w_up, w_gate, w_conv, w_down, ffn_prev)
    return y, new_ffn.transpose(1, 2, 0, 3).reshape(n_seq, CONV_K - 1, dff)


def _layer(x, *, seq_len, attn_fn, conv_prev, ffn_prev, mem_k, mem_v, w, tm_proj, tm_mid, tm_ffn):
    proj, kv_f32 = _proj(x, w["g_norm1"], w["w_in"], w["g_in_cols"], tm=tm_proj, tn=1024,
                         head_dim=HEAD_DIM, norm_cols=2 * ATTN_WIDTH, out_dtype=BF16,
                         aux_cols=(ATTN_WIDTH, 3 * ATTN_WIDTH), name="in_proj")
    a, new_k, new_v = attn_fn(proj, kv_f32)
    h, new_conv = _mid(a.reshape(-1, ATTN_WIDTH), proj, conv_prev, x, mem_k, mem_v,
                       w["w_conv_mix"], w["g_out_attn"], w["g_out_conv"], w["w_out"],
                       w["g_norm2"], w["w_xq"], w["g_xq"], w["w_xo"], tm=tm_mid, seq_len=seq_len)
    y, new_ffn = _ffn(h, w["g_norm3"], w["w_up"], w["w_gate"], w["w_ffn_conv"], w["w_down"],
                      ffn_prev, tm=tm_ffn, tf=512, seq_len=seq_len)
    return y, new_k, new_v, new_conv, new_ffn


def kernel(x_prompt, x_sample, mem_prompt, cache_attn_k, cache_attn_v, cache_conv, cache_ffn_conv, cache_mem_k, cache_mem_v, g_norm1, w_in, g_q, g_k, rel_bias, w_conv_mix, g_out_attn, g_out_conv, w_out, g_norm2, g_mem_norm, w_xq, w_xkv, g_xq, g_xk, w_xo, g_norm3, w_up, w_gate, w_ffn_conv, w_down):
    bp, sp, d = x_prompt.shape
    bs, ss, _ = x_sample.shape
    depth = w_in.shape[0]
    in_width = w_in.shape[2]
    conv_ch = w_conv_mix.shape[2]
    d_ff = w_up.shape[2]
    n_mem = mem_prompt.shape[1]
    n_cache = cache_attn_k.shape[2]
    keep = min(BAND, sp)

    yp = x_prompt.reshape(bp * sp, d)
    ys = x_sample.reshape(bs * ss, d)
    outs = [[] for _ in range(10)]
    for l in range(depth):
        row = lambda g: g[l][None, :]
        w = dict(
            g_norm1=row(g_norm1), w_in=w_in[l].astype(BF16),
            g_in_cols=jnp.concatenate([jnp.tile(g_q[l], N_HEADS), jnp.tile(g_k[l], N_HEADS),
                                       jnp.ones((in_width - 2 * ATTN_WIDTH,), F32)])[None, :],
            w_conv_mix=w_conv_mix[l], g_out_attn=row(g_out_attn), g_out_conv=row(g_out_conv),
            w_out=w_out[l].astype(BF16), g_norm2=row(g_norm2), w_xq=w_xq[l].astype(BF16),
            g_xq=row(g_xq), w_xo=w_xo[l].astype(BF16), g_norm3=row(g_norm3),
            w_up=w_up[l].astype(BF16), w_gate=w_gate[l].astype(BF16),
            w_ffn_conv=w_ffn_conv[l], w_down=w_down[l].astype(BF16))
        pq = _rel_rows(rel_bias[l])

        g_mem_cols = jnp.concatenate([jnp.tile(g_xk[l], X_HEADS), jnp.ones((X_WIDTH,), F32)])[None, :]
        (mem_kv,) = _proj(mem_prompt.reshape(bp * n_mem, d), row(g_mem_norm), w_xkv[l].astype(BF16),
                          g_mem_cols, tm=bp * n_mem, tn=512, head_dim=X_HEAD_DIM,
                          norm_cols=X_WIDTH, out_dtype=F32, name="mem_kv")
        mem_kv = mem_kv.reshape(bp, n_mem, 2 * X_WIDTH)
        mk_p, mv_p = mem_kv[:, :, :X_WIDTH], mem_kv[:, :, X_WIDTH:]

        def prompt_attn(proj, kv_f32):
            a = _prompt_attn(proj.reshape(bp, sp, in_width), pq, batch=bp, seq=sp)
            kv = kv_f32.reshape(bp, sp, 2 * ATTN_WIDTH)[:, sp - keep:]
            return a, kv[:, :, :ATTN_WIDTH], kv[:, :, ATTN_WIDTH:]

        yp, pk, pv, pc, pf = _layer(
            yp, seq_len=sp, attn_fn=prompt_attn,
            conv_prev=jnp.zeros((bp, CONV_K - 1, conv_ch), F32),
            ffn_prev=jnp.zeros((bp, CONV_K - 1, d_ff), F32),
            mem_k=mk_p, mem_v=mv_p, w=w, tm_proj=1024, tm_mid=512, tm_ffn=1024)

        ck = cache_attn_k[l].reshape(bs, n_cache * N_HEADS, HEAD_DIM)
        cv = cache_attn_v[l].reshape(bs, n_cache * N_HEADS, HEAD_DIM)

        def sample_attn(proj, kv_f32):
            return _sample_attn(proj.reshape(bs, ss, in_width),
                                kv_f32.reshape(bs, ss, 2 * ATTN_WIDTH), ck, cv, pq)

        ys, sk, sv, sc, sf = _layer(
            ys, seq_len=ss, attn_fn=sample_attn, conv_prev=cache_conv[l], ffn_prev=cache_ffn_conv[l],
            mem_k=cache_mem_k[l].reshape(bs, n_mem, X_WIDTH),
            mem_v=cache_mem_v[l].reshape(bs, n_mem, X_WIDTH),
            w=w, tm_proj=1024, tm_mid=256, tm_ffn=512)

        for lst, val in zip(outs, (
                pk.reshape(bp, keep, N_HEADS, HEAD_DIM), pv.reshape(bp, keep, N_HEADS, HEAD_DIM),
                pc, pf,
                mk_p.reshape(bp, n_mem, X_HEADS, X_HEAD_DIM), mv_p.reshape(bp, n_mem, X_HEADS, X_HEAD_DIM),
                sk.reshape(bs, n_cache, N_HEADS, HEAD_DIM), sv.reshape(bs, n_cache, N_HEADS, HEAD_DIM),
                sc, sf)):
            lst.append(val)

    return (yp.reshape(bp, sp, d), ys.reshape(bs, ss, d)) + tuple(jnp.stack(o) for o in outs)
```

```python
import functools
import math

import jax
import jax.numpy as jnp
import numpy as np
from jax import lax
from jax.experimental import pallas as pl
from jax.experimental.pallas import tpu as pltpu

CHUNK = 64
N_PREV_CHUNKS = 8
BAND = N_PREV_CHUNKS * CHUNK
N_HEADS = 8
HEAD_DIM = 128
ATTN_WIDTH = N_HEADS * HEAD_DIM
REL_CLIP = 128
CONV_K = 3
X_HEADS = 4
X_HEAD_DIM = 256
X_WIDTH = X_HEADS * X_HEAD_DIM
EPS = 1e-6
NEG_INF = -1e30
LOG2E = math.log2(math.e)

BF16 = jnp.bfloat16
F32 = jnp.float32

V7X_VMEM_BYTES = 64 * 1024 * 1024

KEY_PIECE = 2 * REL_CLIP
ATTN_PIECES = BAND // KEY_PIECE + 1
assert KEY_PIECE % CHUNK == 0 and BAND % KEY_PIECE == 0
ATTN_SUB = 2


def _nbytes(shape, dtype):
    return int(np.prod(shape)) * jnp.dtype(dtype).itemsize


def _vmem_limit(pipelined, resident=(), temps=0):
    est = 2 * sum(_nbytes(s, d) for s, d in pipelined)
    est += sum(_nbytes(s, d) for s, d in resident) + temps
    return min(int(est * 1.25) + (2 << 20), V7X_VMEM_BYTES - (6 << 20))


def _rms(xf, g):
    return xf * lax.rsqrt(jnp.mean(xf * xf, axis=-1, keepdims=True) + EPS) * g


def _dot(a, b):
    return jnp.dot(a, b, preferred_element_type=F32)


def _dot_nt(a, b):
    return lax.dot_general(a, b, (((1,), (1,)), ((), ())), preferred_element_type=F32)


def _dot_tn(a, b):
    return lax.dot_general(a, b, (((0,), (0,)), ((), ())), preferred_element_type=F32)


def _lookup(j, table):
    return functools.reduce(lambda acc, kv: jnp.where(j == kv[0], kv[1], acc),
                            enumerate(table), jnp.int32(0))


def _proj_kernel(x_ref, gin_ref, w_ref, gcol_ref, *rest, head_dim, norm_flags, has_aux,
                 row_chunk):
    if has_aux:
        out_ref, aux_ref, n_scr = rest
    else:
        out_ref, n_scr = rest
    j = pl.program_id(1)
    tn = out_ref.shape[1]

    @pl.when(j == 0)
    def _():
        n_scr[...] = _rms(x_ref[...], gin_ref[...]).astype(BF16)

    is_norm = _lookup(j, norm_flags) != 0
    for c in range(n_scr.shape[0] // row_chunk):
        rows = slice(c * row_chunk, (c + 1) * row_chunk)
        acc = _dot(n_scr[rows, :], w_ref[...])
        for h in range(tn // head_dim):
            sl = slice(h * head_dim, (h + 1) * head_dim)
            a = acc[:, sl]
            r = lax.rsqrt(jnp.mean(a * a, axis=-1, keepdims=True) + EPS)
            y = a * jnp.where(is_norm, r, 1.0) * gcol_ref[:, sl]
            out_ref[rows, sl] = y.astype(out_ref.dtype)
            if has_aux:
                aux_ref[rows, sl] = y


def _proj(x, g_in, w, gcol, *, tm, tn, head_dim, norm_cols, out_dtype, aux_cols=None, name,
          row_chunk=256):
    t, d = x.shape
    n = w.shape[1]
    nj = n // tn
    has_aux = aux_cols is not None
    aux_tiles = tuple(range(aux_cols[0] // tn, aux_cols[1] // tn)) if has_aux else ()
    order = tuple(c for c in range(nj) if c not in aux_tiles) + aux_tiles
    norm_flags = tuple(int(c < norm_cols // tn) for c in order)
    n_aux = len(aux_tiles)
    col = lambda j: _lookup(j, order)
    out_shape = [jax.ShapeDtypeStruct((t, n), out_dtype)]
    out_specs = [pl.BlockSpec((tm, tn), lambda i, j: (i, col(j)))]
    pipelined = [((tm, d), F32), ((d, tn), BF16), ((tm, tn), out_dtype)]
    if has_aux:
        out_shape.append(jax.ShapeDtypeStruct((t, n_aux * tn), F32))
        out_specs.append(pl.BlockSpec(
            (tm, tn), lambda i, j: (i, jnp.maximum(j - (nj - n_aux), 0))))
        pipelined.append(((tm, tn), F32))
    kern = functools.partial(_proj_kernel, head_dim=head_dim, norm_flags=norm_flags,
                             has_aux=has_aux, row_chunk=min(row_chunk, tm))
    return pl.pallas_call(
        kern,
        grid=(t // tm, nj),
        in_specs=[
            pl.BlockSpec((tm, d), lambda i, j: (i, 0)),
            pl.BlockSpec((1, d), lambda i, j: (0, 0)),
            pl.BlockSpec((d, tn), lambda i, j: (0, col(j))),
            pl.BlockSpec((1, tn), lambda i, j: (0, col(j))),
        ],
        out_specs=out_specs,
        out_shape=out_shape,
        scratch_shapes=[pltpu.VMEM((tm, d), BF16)],
        compiler_params=pltpu.CompilerParams(
            dimension_semantics=("arbitrary", "arbitrary"),
            vmem_limit_bytes=_vmem_limit(pipelined, [((tm, d), BF16)],
                                         temps=3 * _nbytes((tm, tn), F32) + _nbytes((tm, d), F32))),
        name=name,
    )(x, g_in, w, gcol)


def _proj_resident_kernel(x_ref, gin_ref, w_ref, gcol_ref, out_ref, *, head_dim, norm_cols,
                          col_tile, row_chunk):
    tm = x_ref.shape[0]
    for c in range(tm // row_chunk):
        rows = slice(c * row_chunk, (c + 1) * row_chunk)
        n = _rms(x_ref[rows, :], gin_ref[...]).astype(BF16)
        for ct in range(out_ref.shape[1] // col_tile):
            acc = _dot(n, w_ref[:, ct * col_tile:(ct + 1) * col_tile])
            for h in range(col_tile // head_dim):
                sl = slice(ct * col_tile + h * head_dim, ct * col_tile + (h + 1) * head_dim)
                y = acc[:, h * head_dim:(h + 1) * head_dim]
                if sl.start < norm_cols:
                    y = _rms(y, gcol_ref[:, sl])
                out_ref[rows, sl] = y.astype(out_ref.dtype)


def _proj_resident(x, g_in, w, gcol, *, tm, head_dim, norm_cols, out_dtype, name,
                   col_tile=1024, row_chunk=256):
    t, d = x.shape
    n = w.shape[1]
    const = lambda shape: pl.BlockSpec(shape, lambda i: (0, 0), pipeline_mode=pl.Buffered(1))
    kern = functools.partial(_proj_resident_kernel, head_dim=head_dim, norm_cols=norm_cols,
                             col_tile=col_tile, row_chunk=row_chunk)
    return pl.pallas_call(
        kern,
        grid=(t // tm,),
        in_specs=[pl.BlockSpec((tm, d), lambda i: (i, 0)), const((1, d)), const((d, n)),
                  const((1, n))],
        out_specs=pl.BlockSpec((tm, n), lambda i: (i, 0)),
        out_shape=jax.ShapeDtypeStruct((t, n), out_dtype),
        compiler_params=pltpu.CompilerParams(
            dimension_semantics=("arbitrary",),
            vmem_limit_bytes=_vmem_limit(
                [((tm, d), F32), ((tm, n), out_dtype)], [((d, n), BF16)],
                temps=_nbytes((row_chunk, d), F32) + 3 * _nbytes((row_chunk, col_tile), F32))),
        name=name,
    )(x, g_in, w, gcol)


def _proj_tail(x, g_in, w, gcol, *, seq_len, keep, cols, tn, head_dim, norm_cols, name):
    t, d = x.shape
    n_seq = t // seq_len
    lo_tile = cols[0] // tn
    nj = (cols[1] - cols[0]) // tn
    last = seq_len // keep - 1
    norm_flags = tuple(int((lo_tile + j) * tn < norm_cols) for j in range(nj))
    kern = functools.partial(_proj_kernel, head_dim=head_dim, norm_flags=norm_flags,
                             has_aux=False, row_chunk=min(256, keep))
    return pl.pallas_call(
        kern,
        grid=(n_seq, nj),
        in_specs=[
            pl.BlockSpec((keep, d), lambda b, j: (b * (seq_len // keep) + last, 0)),
            pl.BlockSpec((1, d), lambda b, j: (0, 0)),
            pl.BlockSpec((d, tn), lambda b, j: (0, lo_tile + j)),
            pl.BlockSpec((1, tn), lambda b, j: (0, lo_tile + j)),
        ],
        out_specs=pl.BlockSpec((keep, tn), lambda b, j: (b, j)),
        out_shape=jax.ShapeDtypeStruct((n_seq * keep, nj * tn), F32),
        scratch_shapes=[pltpu.VMEM((keep, d), BF16)],
        compiler_params=pltpu.CompilerParams(
            dimension_semantics=("arbitrary", "arbitrary"),
            vmem_limit_bytes=_vmem_limit(
                [((keep, d), F32), ((d, tn), BF16), ((keep, tn), F32)], [((keep, d), BF16)],
                temps=3 * _nbytes((keep, tn), F32) + _nbytes((keep, d), F32))),
        name=name,
    )(x, g_in, w, gcol)


def _toeplitz(vec, rows):
    x = jnp.broadcast_to(vec, (rows, KEY_PIECE))
    r = lax.broadcasted_iota(jnp.int32, (rows, KEY_PIECE), 0)
    for b in range((rows - 1).bit_length()):
        x = jnp.where((r >> b) & 1 == 1, pltpu.roll(x, 1 << b, 1), x)
    return x


def _build_band_bias(pq_ref, bias_scr, mask_band, keys_major=False):
    rows = bias_scr.shape[2] if keys_major else bias_scr.shape[1]
    r = lax.broadcasted_iota(jnp.int32, (rows, KEY_PIECE), 0)
    c = lax.broadcasted_iota(jnp.int32, (rows, KEY_PIECE), 1)
    upper = c >= r
    for h in range(N_HEADS):
        top = jnp.broadcast_to(pq_ref[h, 0:1, :], (rows, KEY_PIECE))
        tp = _toeplitz(pq_ref[h, 1:2, :], rows)
        tq = _toeplitz(pq_ref[h, 2:3, :], rows)
        pieces = [top] * (ATTN_PIECES - 2) + [jnp.where(upper, tp, top), jnp.where(upper, tq, tp)]
        for w, piece in enumerate(pieces):
            piece = piece * LOG2E
            if mask_band:
                kc = w * (KEY_PIECE // CHUNK) + c // CHUNK
                qc = r // CHUNK
                piece = jnp.where((kc >= qc) & (kc <= qc + N_PREV_CHUNKS), piece, NEG_INF)
            if keys_major:
                bias_scr[h, w * KEY_PIECE:(w + 1) * KEY_PIECE, :] = piece.T
            else:
                bias_scr[h, :, w * KEY_PIECE:(w + 1) * KEY_PIECE] = piece


def _rel_rows(table):
    h = table.shape[0]
    top = table[:, 2 * REL_CLIP:]
    lo = table[:, :1]
    p = jnp.concatenate([jnp.broadcast_to(top, (h, REL_CLIP + 1)),
                         table[:, 2 * REL_CLIP - 1:REL_CLIP:-1]], axis=1)
    q = jnp.concatenate([table[:, REL_CLIP::-1],
                         jnp.broadcast_to(lo, (h, KEY_PIECE - REL_CLIP - 1))], axis=1)
    return jnp.stack([jnp.broadcast_to(top, (h, KEY_PIECE)), p, q], axis=1).astype(F32)


def _softmax_pv(scores, values):
    widths = {s.shape[1] for s in scores}
    if len(widths) == 1:
        m = jnp.max(functools.reduce(jnp.maximum, scores), axis=-1, keepdims=True)
    else:
        m = functools.reduce(jnp.maximum, [jnp.max(s, axis=-1, keepdims=True) for s in scores])
    ps = [jnp.exp2(s - m) for s in scores]
    if len(widths) == 1:
        l = jnp.sum(functools.reduce(jnp.add, ps), axis=-1, keepdims=True)
    else:
        l = functools.reduce(jnp.add, [jnp.sum(p, axis=-1, keepdims=True) for p in ps])
    o = functools.reduce(jnp.add, [_dot(p.astype(BF16), v) for p, v in zip(ps, values)])
    return o * (1.0 / l)


def _prompt_attn_kernel(pq_ref, q_ref, *rest):
    n_blk = ATTN_PIECES - 1 + ATTN_SUB
    k_refs = rest[:n_blk]
    v_refs = rest[n_blk:2 * n_blk]
    o_ref, bias_scr = rest[2 * n_blk:]
    t = pl.program_id(1)
    qk_scale = HEAD_DIM ** -0.5 * LOG2E

    @pl.when((pl.program_id(0) == 0) & (t == 0))
    def _():
        _build_band_bias(pq_ref, bias_scr, mask_band=True, keys_major=True)

    def attend(seq_start):
        ones = jnp.ones((KEY_PIECE, HEAD_DIM), BF16)
        for u in range(ATTN_SUB):
            first_piece = max(0, ATTN_PIECES - 1 - u) if seq_start else 0
            pieces = range(first_piece, ATTN_PIECES)
            for h in range(N_HEADS):
                sl = slice(h * HEAD_DIM, (h + 1) * HEAD_DIM)
                q = q_ref[0, u * KEY_PIECE:(u + 1) * KEY_PIECE, sl]
                st = [_dot_nt(k_refs[u + w][0, :, sl], q) * qk_scale
                      + bias_scr[h, w * KEY_PIECE:(w + 1) * KEY_PIECE, :] for w in pieces]
                m = jnp.max(functools.reduce(jnp.maximum, st), axis=0, keepdims=True)
                o = None
                for w, s in zip(pieces, st):
                    v_ones = jnp.concatenate([v_refs[u + w][0, :, sl], ones], axis=1)
                    part = _dot_tn(jnp.exp2(s - m).astype(BF16), v_ones)
                    o = part if o is None else o + part
                o_ref[0, u * KEY_PIECE:(u + 1) * KEY_PIECE, sl] = (
                    o[:, :HEAD_DIM] / o[:, HEAD_DIM:]).astype(o_ref.dtype)

    pl.when(t == 0)(functools.partial(attend, True))
    pl.when(t > 0)(functools.partial(attend, False))


def _prompt_attn(proj, pq, *, batch, seq):
    kp = KEY_PIECE
    tq = ATTN_SUB * kp
    n_blk = ATTN_PIECES - 1 + ATTN_SUB

    def kv_spec(p, col):
        return pl.BlockSpec(
            (1, kp, ATTN_WIDTH),
            lambda b, t: (b, jnp.maximum(t * ATTN_SUB - (ATTN_PIECES - 1) + p, 0), col))

    blk = ((1, kp, ATTN_WIDTH), BF16)
    qblk = ((1, tq, ATTN_WIDTH), BF16)
    bias_shape = (N_HEADS, ATTN_PIECES * kp, kp)
    return pl.pallas_call(
        _prompt_attn_kernel,
        grid=(batch, seq // tq),
        in_specs=[pl.BlockSpec(pq.shape, lambda b, t: (0, 0, 0)),
                  pl.BlockSpec((1, tq, ATTN_WIDTH), lambda b, t: (b, t, 0))]
        + [kv_spec(p, 1) for p in range(n_blk)]
        + [kv_spec(p, 2) for p in range(n_blk)],
        out_specs=pl.BlockSpec((1, tq, ATTN_WIDTH), lambda b, t: (b, t, 0)),
        out_shape=jax.ShapeDtypeStruct((batch, seq, ATTN_WIDTH), BF16),
        scratch_shapes=[pltpu.VMEM(bias_shape, F32)],
        compiler_params=pltpu.CompilerParams(
            dimension_semantics=("arbitrary", "arbitrary"),
            vmem_limit_bytes=_vmem_limit([blk] * (2 * n_blk) + [qblk] * 2, [(bias_shape, F32)],
                                         temps=16 * _nbytes((kp, kp), F32))),
        name="prompt_band_attn",
    )(pq, proj, *([proj] * (2 * n_blk)))


def _sample_attn_kernel(pq_ref, q_ref, kn_ref, vn_ref, kf_ref, vf_ref, ck_ref, cv_ref,
                        o_ref, sk_ref, sv_ref, bias_scr):
    qk_scale = HEAD_DIM ** -0.5 * LOG2E
    t_new = q_ref.shape[1]
    n_cache = ck_ref.shape[1] // N_HEADS
    keep = (n_cache - t_new) * N_HEADS

    @pl.when(pl.program_id(0) == 0)
    def _():
        _build_band_bias(pq_ref, bias_scr, mask_band=False)

    for h in range(N_HEADS):
        sl = slice(h * HEAD_DIM, (h + 1) * HEAD_DIM)
        head_rows = pl.ds(h, n_cache, stride=N_HEADS)
        q = q_ref[0, :, sl]
        s_c = _dot_nt(q, ck_ref[0, head_rows, :].astype(BF16)) * qk_scale + bias_scr[h, :, :n_cache]
        s_n = _dot_nt(q, kn_ref[0, :, sl]) * qk_scale + bias_scr[h, :, n_cache:n_cache + t_new]
        o = _softmax_pv([s_c, s_n], [cv_ref[0, head_rows, :].astype(BF16), vn_ref[0, :, sl]])
        o_ref[0, :, sl] = o.astype(o_ref.dtype)
        new_rows = pl.ds(keep + h, t_new, stride=N_HEADS)
        sk_ref[0, new_rows, :] = kf_ref[0, :, sl]
        sv_ref[0, new_rows, :] = vf_ref[0, :, sl]
    sk_ref[0, :keep, :] = ck_ref[0, t_new * N_HEADS:, :]
    sv_ref[0, :keep, :] = cv_ref[0, t_new * N_HEADS:, :]


def _sample_attn(proj, kv_f32, cache_k, cache_v, pq):
    b, t, _ = proj.shape
    lh = cache_k.shape[1]
    assert lh == BAND * N_HEADS and t <= KEY_PIECE
    new_bf = ((1, t, ATTN_WIDTH), BF16)
    new_f = ((1, t, ATTN_WIDTH), F32)
    cache_blk = ((1, lh, HEAD_DIM), F32)
    bias_shape = (N_HEADS, t, ATTN_PIECES * KEY_PIECE)

    def new_spec(col):
        return pl.BlockSpec((1, t, ATTN_WIDTH), lambda i: (i, 0, col))

    cache_spec = pl.BlockSpec((1, lh, HEAD_DIM), lambda i: (i, 0, 0))
    return pl.pallas_call(
        _sample_attn_kernel,
        grid=(b,),
        in_specs=[pl.BlockSpec(pq.shape, lambda i: (0, 0, 0)),
                  new_spec(0), new_spec(1), new_spec(2), new_spec(0), new_spec(1),
                  cache_spec, cache_spec],
        out_specs=[new_spec(0), cache_spec, cache_spec],
        out_shape=[jax.ShapeDtypeStruct((b, t, ATTN_WIDTH), BF16),
                   jax.ShapeDtypeStruct((b, lh, HEAD_DIM), F32),
                   jax.ShapeDtypeStruct((b, lh, HEAD_DIM), F32)],
        scratch_shapes=[pltpu.VMEM(bias_shape, F32)],
        compiler_params=pltpu.CompilerParams(
            dimension_semantics=("arbitrary",),
            vmem_limit_bytes=_vmem_limit(
                [new_bf] * 4 + [new_f] * 2 + [cache_blk] * 4, [(bias_shape, F32)],
                temps=8 * _nbytes((t, BAND), F32) + 2 * _nbytes((BAND, HEAD_DIM), F32))),
        name="sample_band_attn",
    )(pq, proj, proj, proj, kv_f32, kv_f32, cache_k, cache_v)


def _causal_dwconv3(x, prev, w_ref, nseg, lseg):
    c = x.shape[1]
    pos = lax.broadcasted_iota(jnp.int32, (nseg, lseg, c), 1)
    p0 = prev[:, 0:1, :]
    p1 = prev[:, 1:2, :]
    x3 = x.reshape(nseg, lseg, c)
    sh1 = pltpu.roll(x, 1, 0).reshape(nseg, lseg, c)
    sh2 = pltpu.roll(x, 2, 0).reshape(nseg, lseg, c)
    sh1 = jnp.where(pos == 0, p1, sh1)
    sh2 = jnp.where(pos == 0, p0, jnp.where(pos == 1, p1, sh2))
    y = sh2 * w_ref[0:1, :] + sh1 * w_ref[1:2, :] + x3 * w_ref[2:3, :]
    return y.reshape(nseg * lseg, c), x3[:, lseg - (CONV_K - 1):, :]


def _conv_prev(c, row_chunk, lseg, tiles_per_seq, tile_idx, prev_ref, carry, tail):
    if (c * row_chunk) % lseg != 0:
        return tail[None]
    if tiles_per_seq > 1:
        return jnp.where(tile_idx % tiles_per_seq == 0, prev_ref[...], carry[None])
    s0 = c * row_chunk // lseg
    return prev_ref[s0:s0 + max(1, row_chunk // lseg)]


def _mid_kernel(a_ref, b_ref, c_ref, u_ref, prev_ref, x_ref, mk_ref, mv_ref,
                wconv_ref, ga_ref, gc_ref, wout_ref, g2_ref, wxq_ref, gxq_ref, wxo_ref,
                h_ref, newc_ref, mixed_scr, o_scr, carry_scr, *, lseg, tiles_per_seq, row_chunk):
    i = pl.program_id(0)
    tm = x_ref.shape[0]
    seg_per_chunk = max(1, row_chunk // lseg)
    lseg_c = min(row_chunk, lseg)
    sm_scale = X_HEAD_DIM ** -0.5 * LOG2E
    tail = None
    for c in range(tm // row_chunk):
        rows = slice(c * row_chunk, (c + 1) * row_chunk)
        s0 = c * row_chunk // lseg
        mixed_scr[rows, :ATTN_WIDTH] = _rms(a_ref[rows, :].astype(F32), ga_ref[...]).astype(BF16)

        cu = c_ref[rows, :].astype(F32) * u_ref[rows, :].astype(F32)
        prev = _conv_prev(c, row_chunk, lseg, tiles_per_seq, i, prev_ref, carry_scr[...], tail)
        tail = cu[row_chunk - (CONV_K - 1):, :]
        conv, new_conv = _causal_dwconv3(cu, prev, wconv_ref, seg_per_chunk, lseg_c)
        if ((c + 1) * row_chunk) % lseg == 0:
            newc_ref[s0:s0 + seg_per_chunk] = new_conv
        mixed_scr[rows, ATTN_WIDTH:] = _rms(b_ref[rows, :].astype(F32) * conv,
                                            gc_ref[...]).astype(BF16)

        h = x_ref[rows, :] + _dot(mixed_scr[rows, :], wout_ref[...])

        qx = _dot(_rms(h, g2_ref[...]).astype(BF16), wxq_ref[...])
        for hd in range(X_HEADS):
            sl = slice(hd * X_HEAD_DIM, (hd + 1) * X_HEAD_DIM)
            qh = _rms(qx[:, sl], gxq_ref[...]).astype(BF16)
            for s in range(seg_per_chunk):
                srows = slice(s * lseg_c, (s + 1) * lseg_c)
                sc = _dot_nt(qh[srows], mk_ref[s0 + s, :, sl].astype(BF16)) * sm_scale
                o = _softmax_pv([sc], [mv_ref[s0 + s, :, sl].astype(BF16)])
                o_scr[c * row_chunk + s * lseg_c:c * row_chunk + (s + 1) * lseg_c, sl] = o.astype(BF16)

        h_ref[rows, :] = h + _dot(o_scr[rows, :], wxo_ref[...])
    if tiles_per_seq > 1:
        carry_scr[...] = tail


def _mid(a, proj, conv_prev, x, mem_k, mem_v, w_conv, g_a, g_c, w_out, g2, w_xq, g_xq, w_xo,
         *, tm, seq_len, row_chunk=256):
    t, d = x.shape
    cw = conv_prev.shape[2]
    n_mem = mem_k.shape[1]
    lseg = min(tm, seq_len)
    nseg = tm // lseg
    tps = max(1, seq_len // tm)
    row_chunk = min(row_chunk, tm)
    assert lseg % row_chunk == 0 or row_chunk % lseg == 0

    def col_spec(col):
        return pl.BlockSpec((tm, cw), lambda i: (i, col))

    def const_spec(arr):
        return pl.BlockSpec(arr.shape, lambda i: (0,) * arr.ndim, pipeline_mode=pl.Buffered(1))

    seq_spec3 = lambda rows, width: pl.BlockSpec((nseg, rows, width), lambda i: (i // tps, 0, 0))
    kern = functools.partial(_mid_kernel, lseg=lseg, tiles_per_seq=tps, row_chunk=row_chunk)
    pipelined = ([((tm, cw), BF16)] * 4 + [((nseg, 2, cw), F32)] * 2 + [((tm, d), F32)] * 2
                 + [((nseg, n_mem, X_WIDTH), F32)] * 2)
    resident = [(w.shape, w.dtype) for w in (w_out, w_xq, w_xo)]
    resident += [((tm, d), BF16), ((tm, X_WIDTH), BF16)]
    return pl.pallas_call(
        kern,
        grid=(t // tm,),
        in_specs=[
            pl.BlockSpec((tm, ATTN_WIDTH), lambda i: (i, 0)),
            col_spec(3), col_spec(4), col_spec(5),
            seq_spec3(2, cw),
            pl.BlockSpec((tm, d), lambda i: (i, 0)),
            seq_spec3(n_mem, X_WIDTH), seq_spec3(n_mem, X_WIDTH),
            const_spec(w_conv), const_spec(g_a), const_spec(g_c), const_spec(w_out),
            const_spec(g2), const_spec(w_xq), const_spec(g_xq), const_spec(w_xo),
        ],
        out_specs=[pl.BlockSpec((tm, d), lambda i: (i, 0)), seq_spec3(2, cw)],
        out_shape=[jax.ShapeDtypeStruct((t, d), F32),
                   jax.ShapeDtypeStruct(conv_prev.shape, F32)],
        scratch_shapes=[pltpu.VMEM((tm, d), BF16), pltpu.VMEM((tm, X_WIDTH), BF16),
                        pltpu.VMEM((CONV_K - 1, cw), F32)],
        compiler_params=pltpu.CompilerParams(
            dimension_semantics=("arbitrary",),
            vmem_limit_bytes=_vmem_limit(pipelined, resident,
                                         temps=8 * _nbytes((row_chunk, d), F32))),
        name="mid",
    )(a, proj, proj, proj, conv_prev, x, mem_k, mem_v,
      w_conv, g_a, g_c, w_out, g2, w_xq, g_xq, w_xo)


def _ffn_kernel(h_ref, g3_ref, wup_ref, wgate_ref, wconv_ref, wdown_ref, prev_ref,
                y_ref, newf_ref, n3_scr, carry_scr, *, lseg, tiles_per_seq, row_chunk):
    i = pl.program_id(0)
    f = pl.program_id(1)
    tm = h_ref.shape[0]

    @pl.when(f == 0)
    def _():
        h = h_ref[...]
        n3_scr[...] = _rms(h, g3_ref[...]).astype(BF16)
        y_ref[...] = h

    seg_per_chunk = max(1, row_chunk // lseg)
    lseg_c = min(row_chunk, lseg)
    tail = None
    for c in range(tm // row_chunk):
        rows = slice(c * row_chunk, (c + 1) * row_chunk)
        n3 = n3_scr[rows, :]
        up = _dot(n3, wup_ref[...])
        gate = _dot(n3, wgate_ref[...])
        prev = _conv_prev(c, row_chunk, lseg, tiles_per_seq, i, prev_ref, carry_scr[f], tail)
        tail = up[row_chunk - (CONV_K - 1):, :]
        conv, new_ffn = _causal_dwconv3(up, prev, wconv_ref, seg_per_chunk, lseg_c)
        if ((c + 1) * row_chunk) % lseg == 0:
            s0 = c * row_chunk // lseg
            newf_ref[f, s0:s0 + seg_per_chunk] = new_ffn
        act = conv / (1.0 + jnp.exp(-conv)) * gate
        y_ref[rows, :] += _dot(act.astype(BF16), wdown_ref[...])
    if tiles_per_seq > 1:
        carry_scr[f] = tail


def _ffn(h, g3, w_up, w_gate, w_conv, w_down, ffn_prev, *, tm, tf, seq_len, row_chunk=512):
    t, d = h.shape
    dff = w_up.shape[1]
    n_seq = ffn_prev.shape[0]
    lseg = min(tm, seq_len)
    nseg = tm // lseg
    tps = max(1, seq_len // tm)
    nf = dff // tf
    row_chunk = min(row_chunk, tm)
    assert lseg % row_chunk == 0 or row_chunk % lseg == 0
    kern = functools.partial(_ffn_kernel, lseg=lseg, tiles_per_seq=tps, row_chunk=row_chunk)
    prev_spec = pl.BlockSpec((nseg, CONV_K - 1, tf), lambda i, f: (i // tps, 0, f))
    newf_blk = (nf, nseg, CONV_K - 1, tf)
    pipelined = ([((tm, d), F32)] * 2 + [((d, tf), BF16)] * 3 + [((nseg, 2, tf), F32)]
                 + [(newf_blk, F32), ((CONV_K, tf), F32)])
    resident = [((tm, d), BF16), ((nf, CONV_K - 1, tf), F32)]
    y, new_ffn = pl.pallas_call(
        kern,
        grid=(t // tm, nf),
        in_specs=[
            pl.BlockSpec((tm, d), lambda i, f: (i, 0)),
            pl.BlockSpec((1, d), lambda i, f: (0, 0)),
            pl.BlockSpec((d, tf), lambda i, f: (0, f)),
            pl.BlockSpec((d, tf), lambda i, f: (0, f)),
            pl.BlockSpec((CONV_K, tf), lambda i, f: (0, f)),
            pl.BlockSpec((tf, d), lambda i, f: (f, 0)),
            prev_spec,
        ],
        out_specs=[pl.BlockSpec((tm, d), lambda i, f: (i, 0)),
                   pl.BlockSpec(newf_blk, lambda i, f: (0, i // tps, 0, 0))],
        out_shape=[jax.ShapeDtypeStruct((t, d), F32),
                   jax.ShapeDtypeStruct((nf, n_seq, CONV_K - 1, tf), F32)],
        scratch_shapes=[pltpu.VMEM((tm, d), BF16), pltpu.VMEM((nf, CONV_K - 1, tf), F32)],
        compiler_params=pltpu.CompilerParams(
            dimension_semantics=("arbitrary", "arbitrary"),
            vmem_limit_bytes=_vmem_limit(pipelined, resident, temps=10 * _nbytes((tm, tf), F32))),
        name="ffn",
    )(h, g3, w_up, w_gate, w_conv, w_down, ffn_prev)
    return y, new_ffn.transpose(1, 2, 0, 3).reshape(n_seq, CONV_K - 1, dff)


def _layer(x, *, seq_len, proj_fn, attn_fn, conv_prev, ffn_prev, mem_k, mem_v, w, tm_mid, tm_ffn):
    proj, kv_f32 = proj_fn(x)
    a, new_k, new_v = attn_fn(proj, kv_f32)
    h, new_conv = _mid(a.reshape(-1, ATTN_WIDTH), proj, conv_prev, x, mem_k, mem_v,
                       w["w_conv_mix"], w["g_out_attn"], w["g_out_conv"], w["w_out"],
                       w["g_norm2"], w["w_xq"], w["g_xq"], w["w_xo"], tm=tm_mid, seq_len=seq_len)
    y, new_ffn = _ffn(h, w["g_norm3"], w["w_up"], w["w_gate"], w["w_ffn_conv"], w["w_down"],
                      ffn_prev, tm=tm_ffn, tf=512, seq_len=seq_len)
    return y, new_k, new_v, new_conv, new_ffn


def kernel(x_prompt, x_sample, mem_prompt, cache_attn_k, cache_attn_v, cache_conv, cache_ffn_conv, cache_mem_k, cache_mem_v, g_norm1, w_in, g_q, g_k, rel_bias, w_conv_mix, g_out_attn, g_out_conv, w_out, g_norm2, g_mem_norm, w_xq, w_xkv, g_xq, g_xk, w_xo, g_norm3, w_up, w_gate, w_ffn_conv, w_down):
    bp, sp, d = x_prompt.shape
    bs, ss, _ = x_sample.shape
    depth = w_in.shape[0]
    in_width = w_in.shape[2]
    conv_ch = w_conv_mix.shape[2]
    d_ff = w_up.shape[2]
    n_mem = mem_prompt.shape[1]
    n_cache = cache_attn_k.shape[2]
    keep = min(BAND, sp)

    yp = x_prompt.reshape(bp * sp, d)
    ys = x_sample.reshape(bs * ss, d)
    outs = [[] for _ in range(10)]
    for l in range(depth):
        row = lambda g: g[l][None, :]
        w = dict(
            g_norm1=row(g_norm1), w_in=w_in[l].astype(BF16),
            g_in_cols=jnp.concatenate([jnp.tile(g_q[l], N_HEADS), jnp.tile(g_k[l], N_HEADS),
                                       jnp.ones((in_width - 2 * ATTN_WIDTH,), F32)])[None, :],
            w_conv_mix=w_conv_mix[l], g_out_attn=row(g_out_attn), g_out_conv=row(g_out_conv),
            w_out=w_out[l].astype(BF16), g_norm2=row(g_norm2), w_xq=w_xq[l].astype(BF16),
            g_xq=row(g_xq), w_xo=w_xo[l].astype(BF16), g_norm3=row(g_norm3),
            w_up=w_up[l].astype(BF16), w_gate=w_gate[l].astype(BF16),
            w_ffn_conv=w_ffn_conv[l], w_down=w_down[l].astype(BF16))
        pq = _rel_rows(rel_bias[l])

        g_mem_cols = jnp.concatenate([jnp.tile(g_xk[l], X_HEADS), jnp.ones((X_WIDTH,), F32)])[None, :]
        (mem_kv,) = _proj(mem_prompt.reshape(bp * n_mem, d), row(g_mem_norm), w_xkv[l].astype(BF16),
                          g_mem_cols, tm=bp * n_mem, tn=512, head_dim=X_HEAD_DIM,
                          norm_cols=X_WIDTH, out_dtype=F32, name="mem_kv")
        mem_kv = mem_kv.reshape(bp, n_mem, 2 * X_WIDTH)
        mk_p, mv_p = mem_kv[:, :, :X_WIDTH], mem_kv[:, :, X_WIDTH:]

        proj_args = dict(head_dim=HEAD_DIM, norm_cols=2 * ATTN_WIDTH)
        kv_cols = (ATTN_WIDTH, 3 * ATTN_WIDTH)

        def prompt_proj(x):
            proj = _proj_resident(x, w["g_norm1"], w["w_in"], w["g_in_cols"], tm=512,
                                  out_dtype=BF16, name="in_proj", **proj_args)
            kv_tail = _proj_tail(x, w["g_norm1"], w["w_in"], w["g_in_cols"], seq_len=sp,
                                 keep=keep, cols=kv_cols, tn=1024, name="kv_tail", **proj_args)
            return proj, kv_tail

        def prompt_attn(proj, kv_tail):
            a = _prompt_attn(proj.reshape(bp, sp, in_width), pq, batch=bp, seq=sp)
            kv = kv_tail.reshape(bp, keep, 2 * ATTN_WIDTH)
            return a, kv[:, :, :ATTN_WIDTH], kv[:, :, ATTN_WIDTH:]

        yp, pk, pv, pc, pf = _layer(
            yp, seq_len=sp, proj_fn=prompt_proj, attn_fn=prompt_attn,
            conv_prev=jnp.zeros((bp, CONV_K - 1, conv_ch), F32),
            ffn_prev=jnp.zeros((bp, CONV_K - 1, d_ff), F32),
            mem_k=mk_p, mem_v=mv_p, w=w, tm_mid=512, tm_ffn=1024)

        ck = cache_attn_k[l].reshape(bs, n_cache * N_HEADS, HEAD_DIM)
        cv = cache_attn_v[l].reshape(bs, n_cache * N_HEADS, HEAD_DIM)

        def sample_attn(proj, kv_f32):
            return _sample_attn(proj.reshape(bs, ss, in_width),
                                kv_f32.reshape(bs, ss, 2 * ATTN_WIDTH), ck, cv, pq)

        def sample_proj(x):
            return _proj(x, w["g_norm1"], w["w_in"], w["g_in_cols"], tm=1024, tn=1024,
                         out_dtype=BF16, aux_cols=kv_cols, name="in_proj", **proj_args)

        ys, sk, sv, sc, sf = _layer(
            ys, seq_len=ss, proj_fn=sample_proj, attn_fn=sample_attn,
            conv_prev=cache_conv[l], ffn_prev=cache_ffn_conv[l],
            mem_k=cache_mem_k[l].reshape(bs, n_mem, X_WIDTH),
            mem_v=cache_mem_v[l].reshape(bs, n_mem, X_WIDTH),
            w=w, tm_mid=256, tm_ffn=512)

        for lst, val in zip(outs, (
                pk.reshape(bp, keep, N_HEADS, HEAD_DIM), pv.reshape(bp, keep, N_HEADS, HEAD_DIM),
                pc, pf,
                mk_p.reshape(bp, n_mem, X_HEADS, X_HEAD_DIM), mv_p.reshape(bp, n_mem, X_HEADS, X_HEAD_DIM),
                sk.reshape(bs, n_cache, N_HEADS, HEAD_DIM), sv.reshape(bs, n_cache, N_HEADS, HEAD_DIM),
                sc, sf)):
            lst.append(val)

    return (yp.reshape(bp, sp, d), ys.reshape(bs, ss, d)) + tuple(jnp.stack(o) for o in outs)
```

```python
import functools
import math

import jax
import jax.numpy as jnp
import numpy as np
from jax import lax
from jax.experimental import pallas as pl
from jax.experimental.pallas import tpu as pltpu

CHUNK = 64
N_PREV_CHUNKS = 8
BAND = N_PREV_CHUNKS * CHUNK
N_HEADS = 8
HEAD_DIM = 128
ATTN_WIDTH = N_HEADS * HEAD_DIM
REL_CLIP = 128
CONV_K = 3
X_HEADS = 4
X_HEAD_DIM = 256
X_WIDTH = X_HEADS * X_HEAD_DIM
EPS = 1e-6
NEG_INF = -1e30
LOG2E = math.log2(math.e)

BF16 = jnp.bfloat16
F32 = jnp.float32

V7X_VMEM_BYTES = 64 * 1024 * 1024
F32_SUBLANES = 8

KEY_PIECE = 2 * REL_CLIP
ATTN_PIECES = BAND // KEY_PIECE + 1
assert KEY_PIECE % CHUNK == 0 and BAND % KEY_PIECE == 0
ATTN_SUB = 2


def _nbytes(shape, dtype):
    return int(np.prod(shape)) * jnp.dtype(dtype).itemsize


def _vmem_limit(pipelined, resident=(), temps=0):
    est = 2 * sum(_nbytes(s, d) for s, d in pipelined)
    est += sum(_nbytes(s, d) for s, d in resident) + temps
    return min(int(est * 1.25) + (2 << 20), V7X_VMEM_BYTES - (6 << 20))


def _rms(xf, g):
    return xf * lax.rsqrt(jnp.mean(xf * xf, axis=-1, keepdims=True) + EPS) * g


def _dot(a, b):
    return jnp.dot(a, b, preferred_element_type=F32)


def _dot_nt(a, b):
    return lax.dot_general(a, b, (((1,), (1,)), ((), ())), preferred_element_type=F32)


def _dot_tn(a, b):
    return lax.dot_general(a, b, (((0,), (0,)), ((), ())), preferred_element_type=F32)


def _lookup(j, table):
    return functools.reduce(lambda acc, kv: jnp.where(j == kv[0], kv[1], acc),
                            enumerate(table), jnp.int32(0))


def _proj_kernel(x_ref, gin_ref, w_ref, gcol_ref, *rest, head_dim, norm_flags, has_aux,
                 row_chunk):
    if has_aux:
        out_ref, aux_ref, n_scr = rest
    else:
        out_ref, n_scr = rest
    j = pl.program_id(1)
    tn = out_ref.shape[1]

    @pl.when(j == 0)
    def _():
        n_scr[...] = _rms(x_ref[...], gin_ref[...]).astype(BF16)

    is_norm = _lookup(j, norm_flags) != 0
    for c in range(n_scr.shape[0] // row_chunk):
        rows = slice(c * row_chunk, (c + 1) * row_chunk)
        acc = _dot(n_scr[rows, :], w_ref[...])
        for h in range(tn // head_dim):
            sl = slice(h * head_dim, (h + 1) * head_dim)
            a = acc[:, sl]
            r = lax.rsqrt(jnp.mean(a * a, axis=-1, keepdims=True) + EPS)
            y = a * jnp.where(is_norm, r, 1.0) * gcol_ref[:, sl]
            out_ref[rows, sl] = y.astype(out_ref.dtype)
            if has_aux:
                aux_ref[rows, sl] = y


def _proj(x, g_in, w, gcol, *, tm, tn, head_dim, norm_cols, out_dtype, aux_cols=None, name,
          row_chunk=256):
    t, d = x.shape
    n = w.shape[1]
    nj = n // tn
    has_aux = aux_cols is not None
    aux_tiles = tuple(range(aux_cols[0] // tn, aux_cols[1] // tn)) if has_aux else ()
    order = tuple(c for c in range(nj) if c not in aux_tiles) + aux_tiles
    norm_flags = tuple(int(c < norm_cols // tn) for c in order)
    n_aux = len(aux_tiles)
    col = lambda j: _lookup(j, order)
    out_shape = [jax.ShapeDtypeStruct((t, n), out_dtype)]
    out_specs = [pl.BlockSpec((tm, tn), lambda i, j: (i, col(j)))]
    pipelined = [((tm, d), F32), ((d, tn), BF16), ((tm, tn), out_dtype)]
    if has_aux:
        out_shape.append(jax.ShapeDtypeStruct((t, n_aux * tn), F32))
        out_specs.append(pl.BlockSpec(
            (tm, tn), lambda i, j: (i, jnp.maximum(j - (nj - n_aux), 0))))
        pipelined.append(((tm, tn), F32))
    kern = functools.partial(_proj_kernel, head_dim=head_dim, norm_flags=norm_flags,
                             has_aux=has_aux, row_chunk=min(row_chunk, tm))
    return pl.pallas_call(
        kern,
        grid=(t // tm, nj),
        in_specs=[
            pl.BlockSpec((tm, d), lambda i, j: (i, 0)),
            pl.BlockSpec((1, d), lambda i, j: (0, 0)),
            pl.BlockSpec((d, tn), lambda i, j: (0, col(j))),
            pl.BlockSpec((1, tn), lambda i, j: (0, col(j))),
        ],
        out_specs=out_specs,
        out_shape=out_shape,
        scratch_shapes=[pltpu.VMEM((tm, d), BF16)],
        compiler_params=pltpu.CompilerParams(
            dimension_semantics=("arbitrary", "arbitrary"),
            vmem_limit_bytes=_vmem_limit(pipelined, [((tm, d), BF16)],
                                         temps=3 * _nbytes((tm, tn), F32) + _nbytes((tm, d), F32))),
        name=name,
    )(x, g_in, w, gcol)


def _proj_resident_kernel(x_ref, gin_ref, w_ref, gcol_ref, out_ref, *, head_dim, norm_cols,
                          col_tile, row_chunk):
    tm = x_ref.shape[0]
    for c in range(tm // row_chunk):
        rows = slice(c * row_chunk, (c + 1) * row_chunk)
        n = _rms(x_ref[rows, :], gin_ref[...]).astype(BF16)
        for ct in range(out_ref.shape[1] // col_tile):
            acc = _dot(n, w_ref[:, ct * col_tile:(ct + 1) * col_tile])
            for h in range(col_tile // head_dim):
                sl = slice(ct * col_tile + h * head_dim, ct * col_tile + (h + 1) * head_dim)
                y = acc[:, h * head_dim:(h + 1) * head_dim]
                if sl.start < norm_cols:
                    y = _rms(y, gcol_ref[:, sl])
                out_ref[rows, sl] = y.astype(out_ref.dtype)


def _proj_resident(x, g_in, w, gcol, *, tm, head_dim, norm_cols, out_dtype, name,
                   col_tile=1024, row_chunk=256):
    t, d = x.shape
    n = w.shape[1]
    const = lambda shape: pl.BlockSpec(shape, lambda i: (0, 0), pipeline_mode=pl.Buffered(1))
    kern = functools.partial(_proj_resident_kernel, head_dim=head_dim, norm_cols=norm_cols,
                             col_tile=col_tile, row_chunk=row_chunk)
    return pl.pallas_call(
        kern,
        grid=(t // tm,),
        in_specs=[pl.BlockSpec((tm, d), lambda i: (i, 0)), const((1, d)), const((d, n)),
                  const((1, n))],
        out_specs=pl.BlockSpec((tm, n), lambda i: (i, 0)),
        out_shape=jax.ShapeDtypeStruct((t, n), out_dtype),
        compiler_params=pltpu.CompilerParams(
            dimension_semantics=("arbitrary",),
            vmem_limit_bytes=_vmem_limit(
                [((tm, d), F32), ((tm, n), out_dtype)], [((d, n), BF16)],
                temps=_nbytes((row_chunk, d), F32) + 3 * _nbytes((row_chunk, col_tile), F32))),
        name=name,
    )(x, g_in, w, gcol)


def _proj_tail(x, g_in, w, gcol, *, seq_len, keep, cols, tn, head_dim, norm_cols, name):
    t, d = x.shape
    n_seq = t // seq_len
    lo_tile = cols[0] // tn
    nj = (cols[1] - cols[0]) // tn
    last = seq_len // keep - 1
    norm_flags = tuple(int((lo_tile + j) * tn < norm_cols) for j in range(nj))
    kern = functools.partial(_proj_kernel, head_dim=head_dim, norm_flags=norm_flags,
                             has_aux=False, row_chunk=min(256, keep))
    return pl.pallas_call(
        kern,
        grid=(n_seq, nj),
        in_specs=[
            pl.BlockSpec((keep, d), lambda b, j: (b * (seq_len // keep) + last, 0)),
            pl.BlockSpec((1, d), lambda b, j: (0, 0)),
            pl.BlockSpec((d, tn), lambda b, j: (0, lo_tile + j)),
            pl.BlockSpec((1, tn), lambda b, j: (0, lo_tile + j)),
        ],
        out_specs=pl.BlockSpec((keep, tn), lambda b, j: (b, j)),
        out_shape=jax.ShapeDtypeStruct((n_seq * keep, nj * tn), F32),
        scratch_shapes=[pltpu.VMEM((keep, d), BF16)],
        compiler_params=pltpu.CompilerParams(
            dimension_semantics=("arbitrary", "arbitrary"),
            vmem_limit_bytes=_vmem_limit(
                [((keep, d), F32), ((d, tn), BF16), ((keep, tn), F32)], [((keep, d), BF16)],
                temps=3 * _nbytes((keep, tn), F32) + _nbytes((keep, d), F32))),
        name=name,
    )(x, g_in, w, gcol)


def _toeplitz(vec, rows):
    x = jnp.broadcast_to(vec, (rows, KEY_PIECE))
    r = lax.broadcasted_iota(jnp.int32, (rows, KEY_PIECE), 0)
    for b in range((rows - 1).bit_length()):
        x = jnp.where((r >> b) & 1 == 1, pltpu.roll(x, 1 << b, 1), x)
    return x


def _build_band_bias(pq_ref, bias_scr, mask_band, keys_major=False):
    rows = bias_scr.shape[2] if keys_major else bias_scr.shape[1]
    r = lax.broadcasted_iota(jnp.int32, (rows, KEY_PIECE), 0)
    c = lax.broadcasted_iota(jnp.int32, (rows, KEY_PIECE), 1)
    upper = c >= r
    for h in range(N_HEADS):
        top = jnp.broadcast_to(pq_ref[h, 0:1, :], (rows, KEY_PIECE))
        tp = _toeplitz(pq_ref[h, 1:2, :], rows)
        tq = _toeplitz(pq_ref[h, 2:3, :], rows)
        pieces = [top] * (ATTN_PIECES - 2) + [jnp.where(upper, tp, top), jnp.where(upper, tq, tp)]
        for w, piece in enumerate(pieces):
            piece = piece * LOG2E
            if mask_band:
                kc = w * (KEY_PIECE // CHUNK) + c // CHUNK
                qc = r // CHUNK
                piece = jnp.where((kc >= qc) & (kc <= qc + N_PREV_CHUNKS), piece, NEG_INF)
            if keys_major:
                bias_scr[h, w * KEY_PIECE:(w + 1) * KEY_PIECE, :] = piece.T
            else:
                bias_scr[h, :, w * KEY_PIECE:(w + 1) * KEY_PIECE] = piece


def _rel_rows(table):
    h = table.shape[0]
    top = table[:, 2 * REL_CLIP:]
    lo = table[:, :1]
    p = jnp.concatenate([jnp.broadcast_to(top, (h, REL_CLIP + 1)),
                         table[:, 2 * REL_CLIP - 1:REL_CLIP:-1]], axis=1)
    q = jnp.concatenate([table[:, REL_CLIP::-1],
                         jnp.broadcast_to(lo, (h, KEY_PIECE - REL_CLIP - 1))], axis=1)
    return jnp.stack([jnp.broadcast_to(top, (h, KEY_PIECE)), p, q], axis=1).astype(F32)


def _softmax_pv(scores, values):
    widths = {s.shape[1] for s in scores}
    if len(widths) == 1:
        m = jnp.max(functools.reduce(jnp.maximum, scores), axis=-1, keepdims=True)
    else:
        m = functools.reduce(jnp.maximum, [jnp.max(s, axis=-1, keepdims=True) for s in scores])
    ps = [jnp.exp2(s - m) for s in scores]
    if len(widths) == 1:
        l = jnp.sum(functools.reduce(jnp.add, ps), axis=-1, keepdims=True)
    else:
        l = functools.reduce(jnp.add, [jnp.sum(p, axis=-1, keepdims=True) for p in ps])
    o = functools.reduce(jnp.add, [_dot(p.astype(BF16), v) for p, v in zip(ps, values)])
    return o * (1.0 / l)


def _prompt_attn_kernel(pq_ref, q_ref, *rest):
    n_blk = ATTN_PIECES - 1 + ATTN_SUB
    k_refs = rest[:n_blk]
    v_refs = rest[n_blk:2 * n_blk]
    o_ref, bias_scr = rest[2 * n_blk:]
    t = pl.program_id(1)
    qk_scale = HEAD_DIM ** -0.5 * LOG2E

    @pl.when((pl.program_id(0) == 0) & (t == 0))
    def _():
        _build_band_bias(pq_ref, bias_scr, mask_band=True, keys_major=True)

    def attend(seq_start):
        ones = jnp.ones((KEY_PIECE, HEAD_DIM), BF16)
        for u in range(ATTN_SUB):
            first_piece = max(0, ATTN_PIECES - 1 - u) if seq_start else 0
            pieces = range(first_piece, ATTN_PIECES)
            for h in range(N_HEADS):
                sl = slice(h * HEAD_DIM, (h + 1) * HEAD_DIM)
                q = q_ref[0, u * KEY_PIECE:(u + 1) * KEY_PIECE, sl]
                st = [_dot_nt(k_refs[u + w][0, :, sl], q) * qk_scale
                      + bias_scr[h, w * KEY_PIECE:(w + 1) * KEY_PIECE, :] for w in pieces]
                m = jnp.max(functools.reduce(jnp.maximum, st), axis=0, keepdims=True)
                o = None
                for w, s in zip(pieces, st):
                    v_ones = jnp.concatenate([v_refs[u + w][0, :, sl], ones], axis=1)
                    part = _dot_tn(jnp.exp2(s - m).astype(BF16), v_ones)
                    o = part if o is None else o + part
                o_ref[0, u * KEY_PIECE:(u + 1) * KEY_PIECE, sl] = (
                    o[:, :HEAD_DIM] / o[:, HEAD_DIM:]).astype(o_ref.dtype)

    pl.when(t == 0)(functools.partial(attend, True))
    pl.when(t > 0)(functools.partial(attend, False))


def _prompt_attn(proj, pq, *, batch, seq):
    kp = KEY_PIECE
    tq = ATTN_SUB * kp
    n_blk = ATTN_PIECES - 1 + ATTN_SUB

    def kv_spec(p, col):
        return pl.BlockSpec(
            (1, kp, ATTN_WIDTH),
            lambda b, t: (b, jnp.maximum(t * ATTN_SUB - (ATTN_PIECES - 1) + p, 0), col))

    blk = ((1, kp, ATTN_WIDTH), BF16)
    qblk = ((1, tq, ATTN_WIDTH), BF16)
    bias_shape = (N_HEADS, ATTN_PIECES * kp, kp)
    return pl.pallas_call(
        _prompt_attn_kernel,
        grid=(batch, seq // tq),
        in_specs=[pl.BlockSpec(pq.shape, lambda b, t: (0, 0, 0)),
                  pl.BlockSpec((1, tq, ATTN_WIDTH), lambda b, t: (b, t, 0))]
        + [kv_spec(p, 1) for p in range(n_blk)]
        + [kv_spec(p, 2) for p in range(n_blk)],
        out_specs=pl.BlockSpec((1, tq, ATTN_WIDTH), lambda b, t: (b, t, 0)),
        out_shape=jax.ShapeDtypeStruct((batch, seq, ATTN_WIDTH), BF16),
        scratch_shapes=[pltpu.VMEM(bias_shape, F32)],
        compiler_params=pltpu.CompilerParams(
            dimension_semantics=("arbitrary", "arbitrary"),
            vmem_limit_bytes=_vmem_limit([blk] * (2 * n_blk) + [qblk] * 2, [(bias_shape, F32)],
                                         temps=16 * _nbytes((kp, kp), F32))),
        name="prompt_band_attn",
    )(pq, proj, *([proj] * (2 * n_blk)))


def _sample_attn_kernel(pq_ref, q_ref, kn_ref, vn_ref, kf_ref, vf_ref, ck_ref, cv_ref,
                        o_ref, sk_ref, sv_ref, bias_scr):
    qk_scale = HEAD_DIM ** -0.5 * LOG2E
    t_new = q_ref.shape[1]
    n_cache = ck_ref.shape[1] // N_HEADS
    keep = (n_cache - t_new) * N_HEADS

    @pl.when(pl.program_id(0) == 0)
    def _():
        _build_band_bias(pq_ref, bias_scr, mask_band=False)

    for h in range(N_HEADS):
        sl = slice(h * HEAD_DIM, (h + 1) * HEAD_DIM)
        head_rows = pl.ds(h, n_cache, stride=N_HEADS)
        q = q_ref[0, :, sl]
        s_c = _dot_nt(q, ck_ref[0, head_rows, :].astype(BF16)) * qk_scale + bias_scr[h, :, :n_cache]
        s_n = _dot_nt(q, kn_ref[0, :, sl]) * qk_scale + bias_scr[h, :, n_cache:n_cache + t_new]
        o = _softmax_pv([s_c, s_n], [cv_ref[0, head_rows, :].astype(BF16), vn_ref[0, :, sl]])
        o_ref[0, :, sl] = o.astype(o_ref.dtype)
        new_rows = pl.ds(keep + h, t_new, stride=N_HEADS)
        sk_ref[0, new_rows, :] = kf_ref[0, :, sl]
        sv_ref[0, new_rows, :] = vf_ref[0, :, sl]
    sk_ref[0, :keep, :] = ck_ref[0, t_new * N_HEADS:, :]
    sv_ref[0, :keep, :] = cv_ref[0, t_new * N_HEADS:, :]


def _sample_attn(proj, kv_f32, cache_k, cache_v, pq):
    b, t, _ = proj.shape
    lh = cache_k.shape[1]
    assert lh == BAND * N_HEADS and t <= KEY_PIECE
    new_bf = ((1, t, ATTN_WIDTH), BF16)
    new_f = ((1, t, ATTN_WIDTH), F32)
    cache_blk = ((1, lh, HEAD_DIM), F32)
    bias_shape = (N_HEADS, t, ATTN_PIECES * KEY_PIECE)

    def new_spec(col):
        return pl.BlockSpec((1, t, ATTN_WIDTH), lambda i: (i, 0, col))

    cache_spec = pl.BlockSpec((1, lh, HEAD_DIM), lambda i: (i, 0, 0))
    return pl.pallas_call(
        _sample_attn_kernel,
        grid=(b,),
        in_specs=[pl.BlockSpec(pq.shape, lambda i: (0, 0, 0)),
                  new_spec(0), new_spec(1), new_spec(2), new_spec(0), new_spec(1),
                  cache_spec, cache_spec],
        out_specs=[new_spec(0), cache_spec, cache_spec],
        out_shape=[jax.ShapeDtypeStruct((b, t, ATTN_WIDTH), BF16),
                   jax.ShapeDtypeStruct((b, lh, HEAD_DIM), F32),
                   jax.ShapeDtypeStruct((b, lh, HEAD_DIM), F32)],
        scratch_shapes=[pltpu.VMEM(bias_shape, F32)],
        compiler_params=pltpu.CompilerParams(
            dimension_semantics=("arbitrary",),
            vmem_limit_bytes=_vmem_limit(
                [new_bf] * 4 + [new_f] * 2 + [cache_blk] * 4, [(bias_shape, F32)],
                temps=8 * _nbytes((t, BAND), F32) + 2 * _nbytes((BAND, HEAD_DIM), F32))),
        name="sample_band_attn",
    )(pq, proj, proj, proj, kv_f32, kv_f32, cache_k, cache_v)


def _causal_dwconv3(x, prev, w_ref, nseg, lseg):
    c = x.shape[1]
    p0 = prev[:, 0:1, :]
    p1 = prev[:, 1:2, :]
    x3 = x.reshape(nseg, lseg, c)
    sh1 = pltpu.roll(x, 1, 0).reshape(nseg, lseg, c)
    sh2 = pltpu.roll(x, 2, 0).reshape(nseg, lseg, c)
    pos = lax.broadcasted_iota(jnp.int32, (nseg, F32_SUBLANES, c), 1)
    top1 = jnp.where(pos == 0, p1, sh1[:, :F32_SUBLANES])
    top2 = jnp.where(pos == 0, p0, jnp.where(pos == 1, p1, sh2[:, :F32_SUBLANES]))
    sh1 = jnp.concatenate([top1, sh1[:, F32_SUBLANES:]], axis=1)
    sh2 = jnp.concatenate([top2, sh2[:, F32_SUBLANES:]], axis=1)
    y = sh2 * w_ref[0:1, :] + sh1 * w_ref[1:2, :] + x3 * w_ref[2:3, :]
    return y.reshape(nseg * lseg, c), x3[:, lseg - (CONV_K - 1):, :]


def _conv_prev(c, row_chunk, lseg, tiles_per_seq, tile_idx, prev_ref, carry, tail):
    if (c * row_chunk) % lseg != 0:
        return tail[None]
    if tiles_per_seq > 1:
        return jnp.where(tile_idx % tiles_per_seq == 0, prev_ref[...], carry[None])
    s0 = c * row_chunk // lseg
    return prev_ref[s0:s0 + max(1, row_chunk // lseg)]


def _mid_kernel(a_ref, b_ref, c_ref, u_ref, prev_ref, x_ref, mk_ref, mv_ref,
                wconv_ref, ga_ref, gc_ref, wout_ref, g2_ref, wxq_ref, gxq_ref, wxo_ref,
                h_ref, newc_ref, mixed_scr, o_scr, carry_scr, *, lseg, tiles_per_seq, row_chunk):
    i = pl.program_id(0)
    tm = x_ref.shape[0]
    seg_per_chunk = max(1, row_chunk // lseg)
    lseg_c = min(row_chunk, lseg)
    sm_scale = X_HEAD_DIM ** -0.5 * LOG2E
    n_chunks = tm // row_chunk
    state = [dict() for _ in range(n_chunks)]

    def rows_of(c):
        return slice(c * row_chunk, (c + 1) * row_chunk)

    def mix(c):
        rows, st = rows_of(c), state[c]
        s0 = c * row_chunk // lseg
        mixed_scr[rows, :ATTN_WIDTH] = _rms(a_ref[rows, :].astype(F32), ga_ref[...]).astype(BF16)
        cu = c_ref[rows, :].astype(F32) * u_ref[rows, :].astype(F32)
        tail = state[c - 1]["tail"] if c > 0 else None
        prev = _conv_prev(c, row_chunk, lseg, tiles_per_seq, i, prev_ref, carry_scr[...], tail)
        st["tail"] = cu[row_chunk - (CONV_K - 1):, :]
        conv, new_conv = _causal_dwconv3(cu, prev, wconv_ref, seg_per_chunk, lseg_c)
        if ((c + 1) * row_chunk) % lseg == 0:
            newc_ref[s0:s0 + seg_per_chunk] = new_conv
        mixed_scr[rows, ATTN_WIDTH:] = _rms(b_ref[rows, :].astype(F32) * conv,
                                            gc_ref[...]).astype(BF16)

    def out_proj(c):
        rows = rows_of(c)
        h_ref[rows, :] = x_ref[rows, :] + _dot(mixed_scr[rows, :], wout_ref[...])

    def q_proj(c):
        rows, st = rows_of(c), state[c]
        st["qx"] = _dot(_rms(h_ref[rows, :], g2_ref[...]).astype(BF16), wxq_ref[...])

    def cross_attn(c):
        st = state[c]
        s0 = c * row_chunk // lseg
        for hd in range(X_HEADS):
            sl = slice(hd * X_HEAD_DIM, (hd + 1) * X_HEAD_DIM)
            qh = _rms(st["qx"][:, sl], gxq_ref[...]).astype(BF16)
            for s in range(seg_per_chunk):
                srows = slice(s * lseg_c, (s + 1) * lseg_c)
                sc = _dot_nt(qh[srows], mk_ref[s0 + s, :, sl].astype(BF16)) * sm_scale
                o = _softmax_pv([sc], [mv_ref[s0 + s, :, sl].astype(BF16)])
                o_scr[c * row_chunk + s * lseg_c:c * row_chunk + (s + 1) * lseg_c, sl] = o.astype(BF16)

    def x_out(c):
        rows = rows_of(c)
        h_ref[rows, :] += _dot(o_scr[rows, :], wxo_ref[...])

    stages = (mix, out_proj, q_proj, cross_attn, x_out)
    for step in range(n_chunks + len(stages) - 1):
        for k, stage in enumerate(stages):
            c = step - k
            if 0 <= c < n_chunks:
                stage(c)
    if tiles_per_seq > 1:
        carry_scr[...] = state[-1]["tail"]


def _mid(a, proj, conv_prev, x, mem_k, mem_v, w_conv, g_a, g_c, w_out, g2, w_xq, g_xq, w_xo,
         *, tm, seq_len, row_chunk=256):
    t, d = x.shape
    cw = conv_prev.shape[2]
    n_mem = mem_k.shape[1]
    lseg = min(tm, seq_len)
    nseg = tm // lseg
    tps = max(1, seq_len // tm)
    row_chunk = min(row_chunk, tm)
    assert lseg % row_chunk == 0 or row_chunk % lseg == 0

    def col_spec(col):
        return pl.BlockSpec((tm, cw), lambda i: (i, col))

    def const_spec(arr):
        return pl.BlockSpec(arr.shape, lambda i: (0,) * arr.ndim, pipeline_mode=pl.Buffered(1))

    seq_spec3 = lambda rows, width: pl.BlockSpec((nseg, rows, width), lambda i: (i // tps, 0, 0))
    kern = functools.partial(_mid_kernel, lseg=lseg, tiles_per_seq=tps, row_chunk=row_chunk)
    pipelined = ([((tm, cw), BF16)] * 4 + [((nseg, 2, cw), F32)] * 2 + [((tm, d), F32)] * 2
                 + [((nseg, n_mem, X_WIDTH), mem_k.dtype)] * 2)
    resident = [(w.shape, w.dtype) for w in (w_out, w_xq, w_xo)]
    resident += [((tm, d), BF16), ((tm, X_WIDTH), BF16)]
    return pl.pallas_call(
        kern,
        grid=(t // tm,),
        in_specs=[
            pl.BlockSpec((tm, ATTN_WIDTH), lambda i: (i, 0)),
            col_spec(3), col_spec(4), col_spec(5),
            seq_spec3(2, cw),
            pl.BlockSpec((tm, d), lambda i: (i, 0)),
            seq_spec3(n_mem, X_WIDTH), seq_spec3(n_mem, X_WIDTH),
            const_spec(w_conv), const_spec(g_a), const_spec(g_c), const_spec(w_out),
            const_spec(g2), const_spec(w_xq), const_spec(g_xq), const_spec(w_xo),
        ],
        out_specs=[pl.BlockSpec((tm, d), lambda i: (i, 0)), seq_spec3(2, cw)],
        out_shape=[jax.ShapeDtypeStruct((t, d), F32),
                   jax.ShapeDtypeStruct(conv_prev.shape, F32)],
        scratch_shapes=[pltpu.VMEM((tm, d), BF16), pltpu.VMEM((tm, X_WIDTH), BF16),
                        pltpu.VMEM((CONV_K - 1, cw), F32)],
        compiler_params=pltpu.CompilerParams(
            dimension_semantics=("arbitrary",),
            vmem_limit_bytes=_vmem_limit(pipelined, resident,
                                         temps=8 * _nbytes((row_chunk, d), F32))),
        name="mid",
    )(a, proj, proj, proj, conv_prev, x, mem_k, mem_v,
      w_conv, g_a, g_c, w_out, g2, w_xq, g_xq, w_xo)


def _ffn_kernel(h_ref, g3_ref, wup_ref, wgate_ref, wconv_ref, wdown_ref, prev_ref,
                y_ref, newf_ref, n3_scr, carry_scr, *, lseg, tiles_per_seq, row_chunk,
                first_row_chunk):
    i = pl.program_id(0)
    f = pl.program_id(1)
    tm = h_ref.shape[0]

    def body(first, row_chunk):
        seg_per_chunk = max(1, row_chunk // lseg)
        lseg_c = min(row_chunk, lseg)
        tail = None
        for c in range(tm // row_chunk):
            rows = slice(c * row_chunk, (c + 1) * row_chunk)
            if first:
                h = h_ref[rows, :]
                n3 = _rms(h, g3_ref[...]).astype(BF16)
                n3_scr[rows, :] = n3
            else:
                n3 = n3_scr[rows, :]
            up = _dot(n3, wup_ref[...])
            gate = _dot(n3, wgate_ref[...])
            prev = _conv_prev(c, row_chunk, lseg, tiles_per_seq, i, prev_ref, carry_scr[f], tail)
            tail = up[row_chunk - (CONV_K - 1):, :]
            conv, new_ffn = _causal_dwconv3(up, prev, wconv_ref, seg_per_chunk, lseg_c)
            if ((c + 1) * row_chunk) % lseg == 0:
                s0 = c * row_chunk // lseg
                newf_ref[f, s0:s0 + seg_per_chunk] = new_ffn
            act = conv / (1.0 + jnp.exp(-conv)) * gate
            down = _dot(act.astype(BF16), wdown_ref[...])
            if first:
                y_ref[rows, :] = h + down
            else:
                y_ref[rows, :] += down
        if tiles_per_seq > 1:
            carry_scr[f] = tail

    pl.when(f == 0)(functools.partial(body, True, first_row_chunk))
    pl.when(f > 0)(functools.partial(body, False, row_chunk))


def _ffn(h, g3, w_up, w_gate, w_conv, w_down, ffn_prev, *, tm, tf, seq_len, row_chunk=512):
    t, d = h.shape
    dff = w_up.shape[1]
    n_seq = ffn_prev.shape[0]
    lseg = min(tm, seq_len)
    nseg = tm // lseg
    tps = max(1, seq_len // tm)
    nf = dff // tf
    row_chunk = min(row_chunk, tm)
    assert lseg % row_chunk == 0 or row_chunk % lseg == 0
    first_row_chunk = min(row_chunk, 256)
    assert lseg % first_row_chunk == 0 or first_row_chunk % lseg == 0
    kern = functools.partial(_ffn_kernel, lseg=lseg, tiles_per_seq=tps, row_chunk=row_chunk,
                             first_row_chunk=first_row_chunk)
    prev_spec = pl.BlockSpec((nseg, CONV_K - 1, tf), lambda i, f: (i // tps, 0, f))
    newf_blk = (nf, nseg, CONV_K - 1, tf)
    pipelined = ([((tm, d), F32)] * 2 + [((d, tf), BF16)] * 3 + [((nseg, 2, tf), F32)]
                 + [(newf_blk, F32), ((CONV_K, tf), F32)])
    resident = [((tm, d), BF16), ((nf, CONV_K - 1, tf), F32)]
    y, new_ffn = pl.pallas_call(
        kern,
        grid=(t // tm, nf),
        in_specs=[
            pl.BlockSpec((tm, d), lambda i, f: (i, 0)),
            pl.BlockSpec((1, d), lambda i, f: (0, 0)),
            pl.BlockSpec((d, tf), lambda i, f: (0, f)),
            pl.BlockSpec((d, tf), lambda i, f: (0, f)),
            pl.BlockSpec((CONV_K, tf), lambda i, f: (0, f)),
            pl.BlockSpec((tf, d), lambda i, f: (f, 0)),
            prev_spec,
        ],
        out_specs=[pl.BlockSpec((tm, d), lambda i, f: (i, 0)),
                   pl.BlockSpec(newf_blk, lambda i, f: (0, i // tps, 0, 0))],
        out_shape=[jax.ShapeDtypeStruct((t, d), F32),
                   jax.ShapeDtypeStruct((nf, n_seq, CONV_K - 1, tf), F32)],
        scratch_shapes=[pltpu.VMEM((tm, d), BF16), pltpu.VMEM((nf, CONV_K - 1, tf), F32)],
        compiler_params=pltpu.CompilerParams(
            dimension_semantics=("arbitrary", "arbitrary"),
            vmem_limit_bytes=_vmem_limit(pipelined, resident, temps=10 * _nbytes((tm, tf), F32))),
        name="ffn",
    )(h, g3, w_up, w_gate, w_conv, w_down, ffn_prev)
    return y, new_ffn.transpose(1, 2, 0, 3).reshape(n_seq, CONV_K - 1, dff)


def _layer(x, *, seq_len, proj_fn, attn_fn, conv_prev, ffn_prev, mem_k, mem_v, w, tm_mid, tm_ffn):
    proj, kv_f32 = proj_fn(x)
    a, new_k, new_v = attn_fn(proj, kv_f32)
    h, new_conv = _mid(a.reshape(-1, ATTN_WIDTH), proj, conv_prev, x, mem_k, mem_v,
                       w["w_conv_mix"], w["g_out_attn"], w["g_out_conv"], w["w_out"],
                       w["g_norm2"], w["w_xq"], w["g_xq"], w["w_xo"], tm=tm_mid, seq_len=seq_len)
    y, new_ffn = _ffn(h, w["g_norm3"], w["w_up"], w["w_gate"], w["w_ffn_conv"], w["w_down"],
                      ffn_prev, tm=tm_ffn, tf=512, seq_len=seq_len)
    return y, new_k, new_v, new_conv, new_ffn


def kernel(x_prompt, x_sample, mem_prompt, cache_attn_k, cache_attn_v, cache_conv, cache_ffn_conv, cache_mem_k, cache_mem_v, g_norm1, w_in, g_q, g_k, rel_bias, w_conv_mix, g_out_attn, g_out_conv, w_out, g_norm2, g_mem_norm, w_xq, w_xkv, g_xq, g_xk, w_xo, g_norm3, w_up, w_gate, w_ffn_conv, w_down):
    bp, sp, d = x_prompt.shape
    bs, ss, _ = x_sample.shape
    depth = w_in.shape[0]
    in_width = w_in.shape[2]
    conv_ch = w_conv_mix.shape[2]
    d_ff = w_up.shape[2]
    n_mem = mem_prompt.shape[1]
    n_cache = cache_attn_k.shape[2]
    keep = min(BAND, sp)

    yp = x_prompt.reshape(bp * sp, d)
    ys = x_sample.reshape(bs * ss, d)
    outs = [[] for _ in range(10)]
    for l in range(depth):
        row = lambda g: g[l][None, :]
        w = dict(
            g_norm1=row(g_norm1), w_in=w_in[l].astype(BF16),
            g_in_cols=jnp.concatenate([jnp.tile(g_q[l], N_HEADS), jnp.tile(g_k[l], N_HEADS),
                                       jnp.ones((in_width - 2 * ATTN_WIDTH,), F32)])[None, :],
            w_conv_mix=w_conv_mix[l], g_out_attn=row(g_out_attn), g_out_conv=row(g_out_conv),
            w_out=w_out[l].astype(BF16), g_norm2=row(g_norm2), w_xq=w_xq[l].astype(BF16),
            g_xq=row(g_xq), w_xo=w_xo[l].astype(BF16), g_norm3=row(g_norm3),
            w_up=w_up[l].astype(BF16), w_gate=w_gate[l].astype(BF16),
            w_ffn_conv=w_ffn_conv[l], w_down=w_down[l].astype(BF16))
        pq = _rel_rows(rel_bias[l])

        g_mem_cols = jnp.concatenate([jnp.tile(g_xk[l], X_HEADS), jnp.ones((X_WIDTH,), F32)])[None, :]
        (mem_kv,) = _proj(mem_prompt.reshape(bp * n_mem, d), row(g_mem_norm), w_xkv[l].astype(BF16),
                          g_mem_cols, tm=bp * n_mem, tn=512, head_dim=X_HEAD_DIM,
                          norm_cols=X_WIDTH, out_dtype=F32, name="mem_kv")
        mem_kv = mem_kv.reshape(bp, n_mem, 2 * X_WIDTH)
        mk_p, mv_p = mem_kv[:, :, :X_WIDTH], mem_kv[:, :, X_WIDTH:]

        proj_args = dict(head_dim=HEAD_DIM, norm_cols=2 * ATTN_WIDTH)
        kv_cols = (ATTN_WIDTH, 3 * ATTN_WIDTH)

        def prompt_proj(x):
            proj = _proj_resident(x, w["g_norm1"], w["w_in"], w["g_in_cols"], tm=512,
                                  out_dtype=BF16, name="in_proj", **proj_args)
            kv_tail = _proj_tail(x, w["g_norm1"], w["w_in"], w["g_in_cols"], seq_len=sp,
                                 keep=keep, cols=kv_cols, tn=1024, name="kv_tail", **proj_args)
            return proj, kv_tail

        def prompt_attn(proj, kv_tail):
            a = _prompt_attn(proj.reshape(bp, sp, in_width), pq, batch=bp, seq=sp)
            kv = kv_tail.reshape(bp, keep, 2 * ATTN_WIDTH)
            return a, kv[:, :, :ATTN_WIDTH], kv[:, :, ATTN_WIDTH:]

        yp, pk, pv, pc, pf = _layer(
            yp, seq_len=sp, proj_fn=prompt_proj, attn_fn=prompt_attn,
            conv_prev=jnp.zeros((bp, CONV_K - 1, conv_ch), F32),
            ffn_prev=jnp.zeros((bp, CONV_K - 1, d_ff), F32),
            mem_k=mk_p.astype(BF16), mem_v=mv_p.astype(BF16), w=w, tm_mid=512, tm_ffn=1024)

        ck = cache_attn_k[l].reshape(bs, n_cache * N_HEADS, HEAD_DIM)
        cv = cache_attn_v[l].reshape(bs, n_cache * N_HEADS, HEAD_DIM)

        def sample_attn(proj, kv_f32):
            return _sample_attn(proj.reshape(bs, ss, in_width),
                                kv_f32.reshape(bs, ss, 2 * ATTN_WIDTH), ck, cv, pq)

        def sample_proj(x):
            return _proj(x, w["g_norm1"], w["w_in"], w["g_in_cols"], tm=1024, tn=1024,
                         out_dtype=BF16, aux_cols=kv_cols, name="in_proj", **proj_args)

        ys, sk, sv, sc, sf = _layer(
            ys, seq_len=ss, proj_fn=sample_proj, attn_fn=sample_attn,
            conv_prev=cache_conv[l], ffn_prev=cache_ffn_conv[l],
            mem_k=cache_mem_k[l].reshape(bs, n_mem, X_WIDTH).astype(BF16),
            mem_v=cache_mem_v[l].reshape(bs, n_mem, X_WIDTH).astype(BF16),
            w=w, tm_mid=256, tm_ffn=512)

        for lst, val in zip(outs, (
                pk.reshape(bp, keep, N_HEADS, HEAD_DIM), pv.reshape(bp, keep, N_HEADS, HEAD_DIM),
                pc, pf,
                mk_p.reshape(bp, n_mem, X_HEADS, X_HEAD_DIM), mv_p.reshape(bp, n_mem, X_HEADS, X_HEAD_DIM),
                sk.reshape(bs, n_cache, N_HEADS, HEAD_DIM), sv.reshape(bs, n_cache, N_HEADS, HEAD_DIM),
                sc, sf)):
            lst.append(val)

    return (yp.reshape(bp, sp, d), ys.reshape(bs, ss, d)) + tuple(jnp.stack(o) for o in outs)
```

```python
import functools
import math

import jax
import jax.numpy as jnp
import numpy as np
from jax import lax
from jax.experimental import pallas as pl
from jax.experimental.pallas import tpu as pltpu

CHUNK = 64
N_PREV_CHUNKS = 8
BAND = N_PREV_CHUNKS * CHUNK
N_HEADS = 8
HEAD_DIM = 128
ATTN_WIDTH = N_HEADS * HEAD_DIM
REL_CLIP = 128
CONV_K = 3
X_HEADS = 4
X_HEAD_DIM = 256
X_WIDTH = X_HEADS * X_HEAD_DIM
EPS = 1e-6
NEG_INF = -1e30
LOG2E = math.log2(math.e)

BF16 = jnp.bfloat16
F32 = jnp.float32

V7X_VMEM_BYTES = 64 * 1024 * 1024
F32_SUBLANES = 8

KEY_PIECE = 2 * REL_CLIP
ATTN_PIECES = BAND // KEY_PIECE + 1
assert KEY_PIECE % CHUNK == 0 and BAND % KEY_PIECE == 0
ATTN_SUB = 4


def _nbytes(shape, dtype):
    return int(np.prod(shape)) * jnp.dtype(dtype).itemsize


def _vmem_limit(pipelined, resident=(), temps=0):
    est = 2 * sum(_nbytes(s, d) for s, d in pipelined)
    est += sum(_nbytes(s, d) for s, d in resident) + temps
    return min(int(est * 1.25) + (2 << 20), V7X_VMEM_BYTES - (6 << 20))


def _rms(xf, g):
    return xf * lax.rsqrt(jnp.mean(xf * xf, axis=-1, keepdims=True) + EPS) * g


def _dot(a, b):
    return jnp.dot(a, b, preferred_element_type=F32)


def _dot_nt(a, b):
    return lax.dot_general(a, b, (((1,), (1,)), ((), ())), preferred_element_type=F32)


def _dot_tn(a, b):
    return lax.dot_general(a, b, (((0,), (0,)), ((), ())), preferred_element_type=F32)


def _lookup(j, table):
    return functools.reduce(lambda acc, kv: jnp.where(j == kv[0], kv[1], acc),
                            enumerate(table), jnp.int32(0))


def _proj_kernel(x_ref, gin_ref, w_ref, gcol_ref, *rest, head_dim, norm_flags, has_aux,
                 row_chunk):
    if has_aux:
        out_ref, aux_ref, n_scr = rest
    else:
        out_ref, n_scr = rest
    j = pl.program_id(1)
    tn = out_ref.shape[1]

    @pl.when(j == 0)
    def _():
        n_scr[...] = _rms(x_ref[...], gin_ref[...]).astype(BF16)

    is_norm = _lookup(j, norm_flags) != 0
    for c in range(n_scr.shape[0] // row_chunk):
        rows = slice(c * row_chunk, (c + 1) * row_chunk)
        acc = _dot(n_scr[rows, :], w_ref[...])
        for h in range(tn // head_dim):
            sl = slice(h * head_dim, (h + 1) * head_dim)
            a = acc[:, sl]
            r = lax.rsqrt(jnp.mean(a * a, axis=-1, keepdims=True) + EPS)
            y = a * jnp.where(is_norm, r, 1.0) * gcol_ref[:, sl]
            out_ref[rows, sl] = y.astype(out_ref.dtype)
            if has_aux:
                aux_ref[rows, sl] = y


def _proj(x, g_in, w, gcol, *, tm, tn, head_dim, norm_cols, out_dtype, aux_cols=None, name,
          row_chunk=256):
    t, d = x.shape
    n = w.shape[1]
    nj = n // tn
    has_aux = aux_cols is not None
    aux_tiles = tuple(range(aux_cols[0] // tn, aux_cols[1] // tn)) if has_aux else ()
    order = tuple(c for c in range(nj) if c not in aux_tiles) + aux_tiles
    norm_flags = tuple(int(c < norm_cols // tn) for c in order)
    n_aux = len(aux_tiles)
    col = lambda j: _lookup(j, order)
    out_shape = [jax.ShapeDtypeStruct((t, n), out_dtype)]
    out_specs = [pl.BlockSpec((tm, tn), lambda i, j: (i, col(j)))]
    pipelined = [((tm, d), F32), ((d, tn), BF16), ((tm, tn), out_dtype)]
    if has_aux:
        out_shape.append(jax.ShapeDtypeStruct((t, n_aux * tn), F32))
        out_specs.append(pl.BlockSpec(
            (tm, tn), lambda i, j: (i, jnp.maximum(j - (nj - n_aux), 0))))
        pipelined.append(((tm, tn), F32))
    kern = functools.partial(_proj_kernel, head_dim=head_dim, norm_flags=norm_flags,
                             has_aux=has_aux, row_chunk=min(row_chunk, tm))
    return pl.pallas_call(
        kern,
        grid=(t // tm, nj),
        in_specs=[
            pl.BlockSpec((tm, d), lambda i, j: (i, 0)),
            pl.BlockSpec((1, d), lambda i, j: (0, 0)),
            pl.BlockSpec((d, tn), lambda i, j: (0, col(j))),
            pl.BlockSpec((1, tn), lambda i, j: (0, col(j))),
        ],
        out_specs=out_specs,
        out_shape=out_shape,
        scratch_shapes=[pltpu.VMEM((tm, d), BF16)],
        compiler_params=pltpu.CompilerParams(
            dimension_semantics=("arbitrary", "arbitrary"),
            vmem_limit_bytes=_vmem_limit(pipelined, [((tm, d), BF16)],
                                         temps=3 * _nbytes((tm, tn), F32) + _nbytes((tm, d), F32))),
        name=name,
    )(x, g_in, w, gcol)


def _proj_resident_kernel(x_ref, gin_ref, w_ref, gcol_ref, out_ref, tail_ref, *, head_dim,
                          norm_cols, col_tile, row_chunk, tail_cols):
    tm = x_ref.shape[0]
    for c in range(tm // row_chunk):
        rows = slice(c * row_chunk, (c + 1) * row_chunk)
        n = _rms(x_ref[rows, :], gin_ref[...]).astype(BF16)
        for ct in range(out_ref.shape[1] // col_tile):
            acc = _dot(n, w_ref[:, ct * col_tile:(ct + 1) * col_tile])
            for h in range(col_tile // head_dim):
                sl = slice(ct * col_tile + h * head_dim, ct * col_tile + (h + 1) * head_dim)
                y = acc[:, h * head_dim:(h + 1) * head_dim]
                if sl.start < norm_cols:
                    y = _rms(y, gcol_ref[:, sl])
                out_ref[rows, sl] = y.astype(out_ref.dtype)
                if tail_cols[0] <= sl.start < tail_cols[1]:
                    tail_ref[rows, sl.start - tail_cols[0]:sl.stop - tail_cols[0]] = y


def _proj_resident(x, g_in, w, gcol, *, tm, seq_len, tail_cols, head_dim, norm_cols, out_dtype,
                   name, col_tile=1024, row_chunk=256):
    t, d = x.shape
    n = w.shape[1]
    tiles_per_seq = seq_len // tm
    n_tail = tail_cols[1] - tail_cols[0]
    const = lambda shape: pl.BlockSpec(shape, lambda i: (0, 0), pipeline_mode=pl.Buffered(1))
    kern = functools.partial(_proj_resident_kernel, head_dim=head_dim, norm_cols=norm_cols,
                             col_tile=col_tile, row_chunk=row_chunk, tail_cols=tail_cols)
    return pl.pallas_call(
        kern,
        grid=(t // tm,),
        in_specs=[pl.BlockSpec((tm, d), lambda i: (i, 0)), const((1, d)), const((d, n)),
                  const((1, n))],
        out_specs=[pl.BlockSpec((tm, n), lambda i: (i, 0)),
                   pl.BlockSpec((tm, n_tail), lambda i: (i // tiles_per_seq, 0))],
        out_shape=[jax.ShapeDtypeStruct((t, n), out_dtype),
                   jax.ShapeDtypeStruct((t // seq_len * tm, n_tail), F32)],
        compiler_params=pltpu.CompilerParams(
            dimension_semantics=("arbitrary",),
            vmem_limit_bytes=_vmem_limit(
                [((tm, d), F32), ((tm, n), out_dtype), ((tm, n_tail), F32)], [((d, n), BF16)],
                temps=_nbytes((row_chunk, d), F32) + 3 * _nbytes((row_chunk, col_tile), F32))),
        name=name,
    )(x, g_in, w, gcol)


def _toeplitz(vec, rows):
    x = jnp.broadcast_to(vec, (rows, KEY_PIECE))
    r = lax.broadcasted_iota(jnp.int32, (rows, KEY_PIECE), 0)
    for b in range((rows - 1).bit_length()):
        x = jnp.where((r >> b) & 1 == 1, pltpu.roll(x, 1 << b, 1), x)
    return x


def _build_band_bias(pq_ref, bias_scr, mask_band, keys_major=False):
    rows = bias_scr.shape[2] if keys_major else bias_scr.shape[1]
    r = lax.broadcasted_iota(jnp.int32, (rows, KEY_PIECE), 0)
    c = lax.broadcasted_iota(jnp.int32, (rows, KEY_PIECE), 1)
    upper = c >= r
    for h in range(N_HEADS):
        top = jnp.broadcast_to(pq_ref[h, 0:1, :], (rows, KEY_PIECE))
        tp = _toeplitz(pq_ref[h, 1:2, :], rows)
        tq = _toeplitz(pq_ref[h, 2:3, :], rows)
        pieces = [top] * (ATTN_PIECES - 2) + [jnp.where(upper, tp, top), jnp.where(upper, tq, tp)]
        for w, piece in enumerate(pieces):
            piece = piece * LOG2E
            if mask_band:
                kc = w * (KEY_PIECE // CHUNK) + c // CHUNK
                qc = r // CHUNK
                piece = jnp.where((kc >= qc) & (kc <= qc + N_PREV_CHUNKS), piece, NEG_INF)
            if keys_major:
                bias_scr[h, w * KEY_PIECE:(w + 1) * KEY_PIECE, :] = piece.T
            else:
                bias_scr[h, :, w * KEY_PIECE:(w + 1) * KEY_PIECE] = piece


def _rel_rows(table):
    h = table.shape[0]
    top = table[:, 2 * REL_CLIP:]
    lo = table[:, :1]
    p = jnp.concatenate([jnp.broadcast_to(top, (h, REL_CLIP + 1)),
                         table[:, 2 * REL_CLIP - 1:REL_CLIP:-1]], axis=1)
    q = jnp.concatenate([table[:, REL_CLIP::-1],
                         jnp.broadcast_to(lo, (h, KEY_PIECE - REL_CLIP - 1))], axis=1)
    return jnp.stack([jnp.broadcast_to(top, (h, KEY_PIECE)), p, q], axis=1).astype(F32)


def _softmax_pv(scores, values):
    widths = {s.shape[1] for s in scores}
    if len(widths) == 1:
        m = jnp.max(functools.reduce(jnp.maximum, scores), axis=-1, keepdims=True)
    else:
        m = functools.reduce(jnp.maximum, [jnp.max(s, axis=-1, keepdims=True) for s in scores])
    ps = [jnp.exp2(s - m) for s in scores]
    if len(widths) == 1:
        l = jnp.sum(functools.reduce(jnp.add, ps), axis=-1, keepdims=True)
    else:
        l = functools.reduce(jnp.add, [jnp.sum(p, axis=-1, keepdims=True) for p in ps])
    o = functools.reduce(jnp.add, [_dot(p.astype(BF16), v) for p, v in zip(ps, values)])
    return o * (1.0 / l)


def _prompt_attn_kernel(pq_ref, q_ref, *rest):
    n_blk = ATTN_PIECES - 1 + ATTN_SUB
    k_refs = rest[:n_blk]
    v_refs = rest[n_blk:2 * n_blk]
    o_ref, bias_scr = rest[2 * n_blk:]
    t = pl.program_id(1)
    qk_scale = HEAD_DIM ** -0.5 * LOG2E

    @pl.when((pl.program_id(0) == 0) & (t == 0))
    def _():
        _build_band_bias(pq_ref, bias_scr, mask_band=True, keys_major=True)

    def attend(seq_start):
        ones = jnp.ones((KEY_PIECE, HEAD_DIM), BF16)
        for u in range(ATTN_SUB):
            first_piece = max(0, ATTN_PIECES - 1 - u) if seq_start else 0
            pieces = range(first_piece, ATTN_PIECES)
            for h in range(N_HEADS):
                sl = slice(h * HEAD_DIM, (h + 1) * HEAD_DIM)
                q = q_ref[0, u * KEY_PIECE:(u + 1) * KEY_PIECE, sl]
                st = [_dot_nt(k_refs[u + w][0, :, sl], q) * qk_scale
                      + bias_scr[h, w * KEY_PIECE:(w + 1) * KEY_PIECE, :] for w in pieces]
                m = jnp.max(functools.reduce(jnp.maximum, st), axis=0, keepdims=True)
                o = None
                for w, s in zip(pieces, st):
                    v_ones = jnp.concatenate([v_refs[u + w][0, :, sl], ones], axis=1)
                    part = _dot_tn(jnp.exp2(s - m).astype(BF16), v_ones)
                    o = part if o is None else o + part
                o_ref[0, u * KEY_PIECE:(u + 1) * KEY_PIECE, sl] = (
                    o[:, :HEAD_DIM] / o[:, HEAD_DIM:]).astype(o_ref.dtype)

    pl.when(t == 0)(functools.partial(attend, True))
    pl.when(t > 0)(functools.partial(attend, False))


def _prompt_attn(proj, pq, *, batch, seq):
    kp = KEY_PIECE
    tq = ATTN_SUB * kp
    n_blk = ATTN_PIECES - 1 + ATTN_SUB

    def kv_spec(p, col):
        return pl.BlockSpec(
            (1, kp, ATTN_WIDTH),
            lambda b, t: (b, jnp.maximum(t * ATTN_SUB - (ATTN_PIECES - 1) + p, 0), col))

    blk = ((1, kp, ATTN_WIDTH), BF16)
    qblk = ((1, tq, ATTN_WIDTH), BF16)
    bias_shape = (N_HEADS, ATTN_PIECES * kp, kp)
    return pl.pallas_call(
        _prompt_attn_kernel,
        grid=(batch, seq // tq),
        in_specs=[pl.BlockSpec(pq.shape, lambda b, t: (0, 0, 0)),
                  pl.BlockSpec((1, tq, ATTN_WIDTH), lambda b, t: (b, t, 0))]
        + [kv_spec(p, 1) for p in range(n_blk)]
        + [kv_spec(p, 2) for p in range(n_blk)],
        out_specs=pl.BlockSpec((1, tq, ATTN_WIDTH), lambda b, t: (b, t, 0)),
        out_shape=jax.ShapeDtypeStruct((batch, seq, ATTN_WIDTH), BF16),
        scratch_shapes=[pltpu.VMEM(bias_shape, F32)],
        compiler_params=pltpu.CompilerParams(
            dimension_semantics=("arbitrary", "arbitrary"),
            vmem_limit_bytes=_vmem_limit([blk] * (2 * n_blk) + [qblk] * 2, [(bias_shape, F32)],
                                         temps=16 * _nbytes((kp, kp), F32))),
        name="prompt_band_attn",
    )(pq, proj, *([proj] * (2 * n_blk)))


def _sample_attn_kernel(pq_ref, q_ref, kn_ref, vn_ref, kf_ref, vf_ref, ck_ref, cv_ref,
                        o_ref, sk_ref, sv_ref, bias_scr):
    qk_scale = HEAD_DIM ** -0.5 * LOG2E
    t_new = q_ref.shape[1]
    n_cache = ck_ref.shape[1] // N_HEADS
    keep = (n_cache - t_new) * N_HEADS

    @pl.when(pl.program_id(0) == 0)
    def _():
        _build_band_bias(pq_ref, bias_scr, mask_band=False)

    for h in range(N_HEADS):
        sl = slice(h * HEAD_DIM, (h + 1) * HEAD_DIM)
        head_rows = pl.ds(h, n_cache, stride=N_HEADS)
        q = q_ref[0, :, sl]
        s_c = _dot_nt(q, ck_ref[0, head_rows, :].astype(BF16)) * qk_scale + bias_scr[h, :, :n_cache]
        s_n = _dot_nt(q, kn_ref[0, :, sl]) * qk_scale + bias_scr[h, :, n_cache:n_cache + t_new]
        o = _softmax_pv([s_c, s_n], [cv_ref[0, head_rows, :].astype(BF16), vn_ref[0, :, sl]])
        o_ref[0, :, sl] = o.astype(o_ref.dtype)
        new_rows = pl.ds(keep + h, t_new, stride=N_HEADS)
        sk_ref[0, new_rows, :] = kf_ref[0, :, sl]
        sv_ref[0, new_rows, :] = vf_ref[0, :, sl]
    sk_ref[0, :keep, :] = ck_ref[0, t_new * N_HEADS:, :]
    sv_ref[0, :keep, :] = cv_ref[0, t_new * N_HEADS:, :]


def _sample_attn(proj, kv_f32, cache_k, cache_v, pq):
    b, t, _ = proj.shape
    lh = cache_k.shape[1]
    assert lh == BAND * N_HEADS and t <= KEY_PIECE
    new_bf = ((1, t, ATTN_WIDTH), BF16)
    new_f = ((1, t, ATTN_WIDTH), F32)
    cache_blk = ((1, lh, HEAD_DIM), F32)
    bias_shape = (N_HEADS, t, ATTN_PIECES * KEY_PIECE)

    def new_spec(col):
        return pl.BlockSpec((1, t, ATTN_WIDTH), lambda i: (i, 0, col))

    cache_spec = pl.BlockSpec((1, lh, HEAD_DIM), lambda i: (i, 0, 0))
    return pl.pallas_call(
        _sample_attn_kernel,
        grid=(b,),
        in_specs=[pl.BlockSpec(pq.shape, lambda i: (0, 0, 0)),
                  new_spec(0), new_spec(1), new_spec(2), new_spec(0), new_spec(1),
                  cache_spec, cache_spec],
        out_specs=[new_spec(0), cache_spec, cache_spec],
        out_shape=[jax.ShapeDtypeStruct((b, t, ATTN_WIDTH), BF16),
                   jax.ShapeDtypeStruct((b, lh, HEAD_DIM), F32),
                   jax.ShapeDtypeStruct((b, lh, HEAD_DIM), F32)],
        scratch_shapes=[pltpu.VMEM(bias_shape, F32)],
        compiler_params=pltpu.CompilerParams(
            dimension_semantics=("arbitrary",),
            vmem_limit_bytes=_vmem_limit(
                [new_bf] * 4 + [new_f] * 2 + [cache_blk] * 4, [(bias_shape, F32)],
                temps=8 * _nbytes((t, BAND), F32) + 2 * _nbytes((BAND, HEAD_DIM), F32))),
        name="sample_band_attn",
    )(pq, proj, proj, proj, kv_f32, kv_f32, cache_k, cache_v)


def _causal_dwconv3(x, prev, w_ref, nseg, lseg):
    c = x.shape[1]
    p0 = prev[:, 0:1, :]
    p1 = prev[:, 1:2, :]
    x3 = x.reshape(nseg, lseg, c)
    sh1 = pltpu.roll(x, 1, 0).reshape(nseg, lseg, c)
    sh2 = pltpu.roll(x, 2, 0).reshape(nseg, lseg, c)
    pos = lax.broadcasted_iota(jnp.int32, (nseg, F32_SUBLANES, c), 1)
    top1 = jnp.where(pos == 0, p1, sh1[:, :F32_SUBLANES])
    top2 = jnp.where(pos == 0, p0, jnp.where(pos == 1, p1, sh2[:, :F32_SUBLANES]))
    sh1 = jnp.concatenate([top1, sh1[:, F32_SUBLANES:]], axis=1)
    sh2 = jnp.concatenate([top2, sh2[:, F32_SUBLANES:]], axis=1)
    y = sh2 * w_ref[0:1, :] + sh1 * w_ref[1:2, :] + x3 * w_ref[2:3, :]
    return y.reshape(nseg * lseg, c), x3[:, lseg - (CONV_K - 1):, :]


def _conv_prev(c, row_chunk, lseg, tiles_per_seq, tile_idx, prev_ref, carry, tail):
    if (c * row_chunk) % lseg != 0:
        return tail[None]
    if tiles_per_seq > 1:
        return jnp.where(tile_idx % tiles_per_seq == 0, prev_ref[...], carry[None])
    s0 = c * row_chunk // lseg
    return prev_ref[s0:s0 + max(1, row_chunk // lseg)]


def _mid_kernel(a_ref, b_ref, c_ref, u_ref, prev_ref, x_ref, mk_ref, mv_ref,
                wconv_ref, ga_ref, gc_ref, wout_ref, g2_ref, wxq_ref, gxq_ref, wxo_ref,
                h_ref, newc_ref, mixed_scr, o_scr, carry_scr, *, lseg, tiles_per_seq, row_chunk):
    i = pl.program_id(0)
    tm = x_ref.shape[0]
    seg_per_chunk = max(1, row_chunk // lseg)
    lseg_c = min(row_chunk, lseg)
    sm_scale = X_HEAD_DIM ** -0.5 * LOG2E
    n_chunks = tm // row_chunk
    state = [dict() for _ in range(n_chunks)]

    def rows_of(c):
        return slice(c * row_chunk, (c + 1) * row_chunk)

    def mix(c):
        rows, st = rows_of(c), state[c]
        s0 = c * row_chunk // lseg
        mixed_scr[rows, :ATTN_WIDTH] = _rms(a_ref[rows, :].astype(F32), ga_ref[...]).astype(BF16)
        cu = c_ref[rows, :].astype(F32) * u_ref[rows, :].astype(F32)
        tail = state[c - 1]["tail"] if c > 0 else None
        prev = _conv_prev(c, row_chunk, lseg, tiles_per_seq, i, prev_ref, carry_scr[...], tail)
        st["tail"] = cu[row_chunk - (CONV_K - 1):, :]
        conv, new_conv = _causal_dwconv3(cu, prev, wconv_ref, seg_per_chunk, lseg_c)
        if ((c + 1) * row_chunk) % lseg == 0:
            newc_ref[s0:s0 + seg_per_chunk] = new_conv
        mixed_scr[rows, ATTN_WIDTH:] = _rms(b_ref[rows, :].astype(F32) * conv,
                                            gc_ref[...]).astype(BF16)

    def out_proj(c):
        rows = rows_of(c)
        h_ref[rows, :] = x_ref[rows, :] + _dot(mixed_scr[rows, :], wout_ref[...])

    def q_proj(c):
        rows, st = rows_of(c), state[c]
        st["qx"] = _dot(_rms(h_ref[rows, :], g2_ref[...]).astype(BF16), wxq_ref[...])

    def cross_attn(c):
        st = state[c]
        s0 = c * row_chunk // lseg
        for hd in range(X_HEADS):
            sl = slice(hd * X_HEAD_DIM, (hd + 1) * X_HEAD_DIM)
            qh = _rms(st["qx"][:, sl], gxq_ref[...]).astype(BF16)
            for s in range(seg_per_chunk):
                srows = slice(s * lseg_c, (s + 1) * lseg_c)
                sc = _dot_nt(qh[srows], mk_ref[s0 + s, :, sl].astype(BF16)) * sm_scale
                o = _softmax_pv([sc], [mv_ref[s0 + s, :, sl].astype(BF16)])
                o_scr[c * row_chunk + s * lseg_c:c * row_chunk + (s + 1) * lseg_c, sl] = o.astype(BF16)

    def x_out(c):
        rows = rows_of(c)
        h_ref[rows, :] += _dot(o_scr[rows, :], wxo_ref[...])

    stages = (mix, out_proj, q_proj, cross_attn, x_out)
    for step in range(n_chunks + len(stages) - 1):
        for k, stage in enumerate(stages):
            c = step - k
            if 0 <= c < n_chunks:
                stage(c)
    if tiles_per_seq > 1:
        carry_scr[...] = state[-1]["tail"]


def _mid(a, proj, conv_prev, x, mem_k, mem_v, w_conv, g_a, g_c, w_out, g2, w_xq, g_xq, w_xo,
         *, tm, seq_len, row_chunk=256):
    t, d = x.shape
    cw = conv_prev.shape[2]
    n_mem = mem_k.shape[1]
    lseg = min(tm, seq_len)
    nseg = tm // lseg
    tps = max(1, seq_len // tm)
    row_chunk = min(row_chunk, tm)
    assert lseg % row_chunk == 0 or row_chunk % lseg == 0

    def col_spec(col):
        return pl.BlockSpec((tm, cw), lambda i: (i, col))

    def const_spec(arr):
        return pl.BlockSpec(arr.shape, lambda i: (0,) * arr.ndim, pipeline_mode=pl.Buffered(1))

    seq_spec3 = lambda rows, width: pl.BlockSpec((nseg, rows, width), lambda i: (i // tps, 0, 0))
    kern = functools.partial(_mid_kernel, lseg=lseg, tiles_per_seq=tps, row_chunk=row_chunk)
    pipelined = ([((tm, cw), BF16)] * 4 + [((nseg, 2, cw), F32)] * 2 + [((tm, d), F32)] * 2
                 + [((nseg, n_mem, X_WIDTH), mem_k.dtype)] * 2)
    resident = [(w.shape, w.dtype) for w in (w_out, w_xq, w_xo)]
    resident += [((tm, d), BF16), ((tm, X_WIDTH), BF16)]
    return pl.pallas_call(
        kern,
        grid=(t // tm,),
        in_specs=[
            pl.BlockSpec((tm, ATTN_WIDTH), lambda i: (i, 0)),
            col_spec(3), col_spec(4), col_spec(5),
            seq_spec3(2, cw),
            pl.BlockSpec((tm, d), lambda i: (i, 0)),
            seq_spec3(n_mem, X_WIDTH), seq_spec3(n_mem, X_WIDTH),
            const_spec(w_conv), const_spec(g_a), const_spec(g_c), const_spec(w_out),
            const_spec(g2), const_spec(w_xq), const_spec(g_xq), const_spec(w_xo),
        ],
        out_specs=[pl.BlockSpec((tm, d), lambda i: (i, 0)), seq_spec3(2, cw)],
        out_shape=[jax.ShapeDtypeStruct((t, d), F32),
                   jax.ShapeDtypeStruct(conv_prev.shape, F32)],
        scratch_shapes=[pltpu.VMEM((tm, d), BF16), pltpu.VMEM((tm, X_WIDTH), BF16),
                        pltpu.VMEM((CONV_K - 1, cw), F32)],
        compiler_params=pltpu.CompilerParams(
            dimension_semantics=("arbitrary",),
            vmem_limit_bytes=_vmem_limit(pipelined, resident,
                                         temps=8 * _nbytes((row_chunk, d), F32))),
        name="mid",
    )(a, proj, proj, proj, conv_prev, x, mem_k, mem_v,
      w_conv, g_a, g_c, w_out, g2, w_xq, g_xq, w_xo)


def _ffn_kernel(h_ref, g3_ref, wup_ref, wgate_ref, wconv_ref, wdown_ref, prev_ref,
                y_ref, newf_ref, n3_scr, carry_scr, *, lseg, tiles_per_seq, row_chunk,
                first_row_chunk):
    i = pl.program_id(0)
    f = pl.program_id(1)
    tm = h_ref.shape[0]

    def body(first, row_chunk):
        seg_per_chunk = max(1, row_chunk // lseg)
        lseg_c = min(row_chunk, lseg)
        tail = None
        for c in range(tm // row_chunk):
            rows = slice(c * row_chunk, (c + 1) * row_chunk)
            if first:
                h = h_ref[rows, :]
                n3 = _rms(h, g3_ref[...]).astype(BF16)
                n3_scr[rows, :] = n3
            else:
                n3 = n3_scr[rows, :]
            up = _dot(n3, wup_ref[...])
            gate = _dot(n3, wgate_ref[...])
            prev = _conv_prev(c, row_chunk, lseg, tiles_per_seq, i, prev_ref, carry_scr[f], tail)
            tail = up[row_chunk - (CONV_K - 1):, :]
            conv, new_ffn = _causal_dwconv3(up, prev, wconv_ref, seg_per_chunk, lseg_c)
            if ((c + 1) * row_chunk) % lseg == 0:
                s0 = c * row_chunk // lseg
                newf_ref[f, s0:s0 + seg_per_chunk] = new_ffn
            act = conv / (1.0 + jnp.exp(-conv)) * gate
            down = _dot(act.astype(BF16), wdown_ref[...])
            if first:
                y_ref[rows, :] = h + down
            else:
                y_ref[rows, :] += down
        if tiles_per_seq > 1:
            carry_scr[f] = tail

    pl.when(f == 0)(functools.partial(body, True, first_row_chunk))
    pl.when(f > 0)(functools.partial(body, False, row_chunk))


def _ffn(h, g3, w_up, w_gate, w_conv, w_down, ffn_prev, *, tm, tf, seq_len, row_chunk=512):
    t, d = h.shape
    dff = w_up.shape[1]
    n_seq = ffn_prev.shape[0]
    lseg = min(tm, seq_len)
    nseg = tm // lseg
    tps = max(1, seq_len // tm)
    nf = dff // tf
    row_chunk = min(row_chunk, tm)
    assert lseg % row_chunk == 0 or row_chunk % lseg == 0
    first_row_chunk = min(row_chunk, 256)
    assert lseg % first_row_chunk == 0 or first_row_chunk % lseg == 0
    kern = functools.partial(_ffn_kernel, lseg=lseg, tiles_per_seq=tps, row_chunk=row_chunk,
                             first_row_chunk=first_row_chunk)
    prev_spec = pl.BlockSpec((nseg, CONV_K - 1, tf), lambda i, f: (i // tps, 0, f))
    newf_blk = (nf, nseg, CONV_K - 1, tf)
    pipelined = ([((tm, d), F32)] * 2 + [((d, tf), BF16)] * 3 + [((nseg, 2, tf), F32)]
                 + [(newf_blk, F32), ((CONV_K, tf), F32)])
    resident = [((tm, d), BF16), ((nf, CONV_K - 1, tf), F32)]
    y, new_ffn = pl.pallas_call(
        kern,
        grid=(t // tm, nf),
        in_specs=[
            pl.BlockSpec((tm, d), lambda i, f: (i, 0)),
            pl.BlockSpec((1, d), lambda i, f: (0, 0)),
            pl.BlockSpec((d, tf), lambda i, f: (0, f)),
            pl.BlockSpec((d, tf), lambda i, f: (0, f)),
            pl.BlockSpec((CONV_K, tf), lambda i, f: (0, f)),
            pl.BlockSpec((tf, d), lambda i, f: (f, 0)),
            prev_spec,
        ],
        out_specs=[pl.BlockSpec((tm, d), lambda i, f: (i, 0)),
                   pl.BlockSpec(newf_blk, lambda i, f: (0, i // tps, 0, 0))],
        out_shape=[jax.ShapeDtypeStruct((t, d), F32),
                   jax.ShapeDtypeStruct((nf, n_seq, CONV_K - 1, tf), F32)],
        scratch_shapes=[pltpu.VMEM((tm, d), BF16), pltpu.VMEM((nf, CONV_K - 1, tf), F32)],
        compiler_params=pltpu.CompilerParams(
            dimension_semantics=("arbitrary", "arbitrary"),
            vmem_limit_bytes=_vmem_limit(pipelined, resident, temps=10 * _nbytes((tm, tf), F32))),
        name="ffn",
    )(h, g3, w_up, w_gate, w_conv, w_down, ffn_prev)
    return y, new_ffn.transpose(1, 2, 0, 3).reshape(n_seq, CONV_K - 1, dff)


def _layer(x, *, seq_len, proj_fn, attn_fn, conv_prev, ffn_prev, mem_k, mem_v, w, tm_mid, tm_ffn):
    proj, kv_f32 = proj_fn(x)
    a, new_k, new_v = attn_fn(proj, kv_f32)
    h, new_conv = _mid(a.reshape(-1, ATTN_WIDTH), proj, conv_prev, x, mem_k, mem_v,
                       w["w_conv_mix"], w["g_out_attn"], w["g_out_conv"], w["w_out"],
                       w["g_norm2"], w["w_xq"], w["g_xq"], w["w_xo"], tm=tm_mid, seq_len=seq_len)
    y, new_ffn = _ffn(h, w["g_norm3"], w["w_up"], w["w_gate"], w["w_ffn_conv"], w["w_down"],
                      ffn_prev, tm=tm_ffn, tf=512, seq_len=seq_len)
    return y, new_k, new_v, new_conv, new_ffn


def kernel(x_prompt, x_sample, mem_prompt, cache_attn_k, cache_attn_v, cache_conv, cache_ffn_conv, cache_mem_k, cache_mem_v, g_norm1, w_in, g_q, g_k, rel_bias, w_conv_mix, g_out_attn, g_out_conv, w_out, g_norm2, g_mem_norm, w_xq, w_xkv, g_xq, g_xk, w_xo, g_norm3, w_up, w_gate, w_ffn_conv, w_down):
    bp, sp, d = x_prompt.shape
    bs, ss, _ = x_sample.shape
    depth = w_in.shape[0]
    in_width = w_in.shape[2]
    conv_ch = w_conv_mix.shape[2]
    d_ff = w_up.shape[2]
    n_mem = mem_prompt.shape[1]
    n_cache = cache_attn_k.shape[2]
    keep = min(BAND, sp)

    yp = x_prompt.reshape(bp * sp, d)
    ys = x_sample.reshape(bs * ss, d)
    outs = [[] for _ in range(10)]
    for l in range(depth):
        row = lambda g: g[l][None, :]
        w = dict(
            g_norm1=row(g_norm1), w_in=w_in[l].astype(BF16),
            g_in_cols=jnp.concatenate([jnp.tile(g_q[l], N_HEADS), jnp.tile(g_k[l], N_HEADS),
                                       jnp.ones((in_width - 2 * ATTN_WIDTH,), F32)])[None, :],
            w_conv_mix=w_conv_mix[l], g_out_attn=row(g_out_attn), g_out_conv=row(g_out_conv),
            w_out=w_out[l].astype(BF16), g_norm2=row(g_norm2), w_xq=w_xq[l].astype(BF16),
            g_xq=row(g_xq), w_xo=w_xo[l].astype(BF16), g_norm3=row(g_norm3),
            w_up=w_up[l].astype(BF16), w_gate=w_gate[l].astype(BF16),
            w_ffn_conv=w_ffn_conv[l], w_down=w_down[l].astype(BF16))
        pq = _rel_rows(rel_bias[l])

        g_mem_cols = jnp.concatenate([jnp.tile(g_xk[l], X_HEADS), jnp.ones((X_WIDTH,), F32)])[None, :]
        (mem_kv,) = _proj(mem_prompt.reshape(bp * n_mem, d), row(g_mem_norm), w_xkv[l].astype(BF16),
                          g_mem_cols, tm=bp * n_mem, tn=512, head_dim=X_HEAD_DIM,
                          norm_cols=X_WIDTH, out_dtype=F32, name="mem_kv")
        mem_kv = mem_kv.reshape(bp, n_mem, 2 * X_WIDTH)
        mk_p, mv_p = mem_kv[:, :, :X_WIDTH], mem_kv[:, :, X_WIDTH:]

        proj_args = dict(head_dim=HEAD_DIM, norm_cols=2 * ATTN_WIDTH)
        kv_cols = (ATTN_WIDTH, 3 * ATTN_WIDTH)

        def prompt_proj(x):
            return _proj_resident(x, w["g_norm1"], w["w_in"], w["g_in_cols"], tm=keep, seq_len=sp,
                                  tail_cols=kv_cols, out_dtype=BF16, name="in_proj", **proj_args)

        def prompt_attn(proj, kv_tail):
            a = _prompt_attn(proj.reshape(bp, sp, in_width), pq, batch=bp, seq=sp)
            kv = kv_tail.reshape(bp, keep, 2 * ATTN_WIDTH)
            return a, kv[:, :, :ATTN_WIDTH], kv[:, :, ATTN_WIDTH:]

        yp, pk, pv, pc, pf = _layer(
            yp, seq_len=sp, proj_fn=prompt_proj, attn_fn=prompt_attn,
            conv_prev=jnp.zeros((bp, CONV_K - 1, conv_ch), F32),
            ffn_prev=jnp.zeros((bp, CONV_K - 1, d_ff), F32),
            mem_k=mk_p.astype(BF16), mem_v=mv_p.astype(BF16), w=w, tm_mid=512, tm_ffn=1024)

        ck = cache_attn_k[l].reshape(bs, n_cache * N_HEADS, HEAD_DIM)
        cv = cache_attn_v[l].reshape(bs, n_cache * N_HEADS, HEAD_DIM)

        def sample_attn(proj, kv_f32):
            return _sample_attn(proj.reshape(bs, ss, in_width),
                                kv_f32.reshape(bs, ss, 2 * ATTN_WIDTH), ck, cv, pq)

        def sample_proj(x):
            return _proj(x, w["g_norm1"], w["w_in"], w["g_in_cols"], tm=1024, tn=1024,
                         out_dtype=BF16, aux_cols=kv_cols, name="in_proj", **proj_args)

        ys, sk, sv, sc, sf = _layer(
            ys, seq_len=ss, proj_fn=sample_proj, attn_fn=sample_attn,
            conv_prev=cache_conv[l], ffn_prev=cache_ffn_conv[l],
            mem_k=cache_mem_k[l].reshape(bs, n_mem, X_WIDTH),
            mem_v=cache_mem_v[l].reshape(bs, n_mem, X_WIDTH),
            w=w, tm_mid=256, tm_ffn=512)

        for lst, val in zip(outs, (
                pk.reshape(bp, keep, N_HEADS, HEAD_DIM), pv.reshape(bp, keep, N_HEADS, HEAD_DIM),
                pc, pf,
                mk_p.reshape(bp, n_mem, X_HEADS, X_HEAD_DIM), mv_p.reshape(bp, n_mem, X_HEADS, X_HEAD_DIM),
                sk.reshape(bs, n_cache, N_HEADS, HEAD_DIM), sv.reshape(bs, n_cache, N_HEADS, HEAD_DIM),
                sc, sf)):
            lst.append(val)

    return (yp.reshape(bp, sp, d), ys.reshape(bs, ss, d)) + tuple(jnp.stack(o) for o in outs)
```

```python
import functools
import math

import jax
import jax.numpy as jnp
import numpy as np
from jax import lax
from jax.experimental import pallas as pl
from jax.experimental.pallas import tpu as pltpu

CHUNK = 64
N_PREV_CHUNKS = 8
BAND = N_PREV_CHUNKS * CHUNK
N_HEADS = 8
HEAD_DIM = 128
ATTN_WIDTH = N_HEADS * HEAD_DIM
REL_CLIP = 128
CONV_K = 3
X_HEADS = 4
X_HEAD_DIM = 256
X_WIDTH = X_HEADS * X_HEAD_DIM
EPS = 1e-6
NEG_INF = -1e30
LOG2E = math.log2(math.e)

BF16 = jnp.bfloat16
F32 = jnp.float32

V7X_VMEM_BYTES = 64 * 1024 * 1024
F32_SUBLANES = 8

KEY_PIECE = 2 * REL_CLIP
ATTN_PIECES = BAND // KEY_PIECE + 1
assert KEY_PIECE % CHUNK == 0 and BAND % KEY_PIECE == 0
ATTN_SUB = 4


def _nbytes(shape, dtype):
    return int(np.prod(shape)) * jnp.dtype(dtype).itemsize


def _vmem_limit(pipelined, resident=(), temps=0):
    est = 2 * sum(_nbytes(s, d) for s, d in pipelined)
    est += sum(_nbytes(s, d) for s, d in resident) + temps
    return min(int(est * 1.25) + (2 << 20), V7X_VMEM_BYTES - (6 << 20))


def _rms(xf, g):
    return xf * lax.rsqrt(jnp.mean(xf * xf, axis=-1, keepdims=True) + EPS) * g


def _dot(a, b):
    return jnp.dot(a, b, preferred_element_type=F32)


def _dot_nt(a, b):
    return lax.dot_general(a, b, (((1,), (1,)), ((), ())), preferred_element_type=F32)


def _dot_tn(a, b):
    return lax.dot_general(a, b, (((0,), (0,)), ((), ())), preferred_element_type=F32)


def _proj_kernel(x_ref, gin_ref, w_ref, gcol_ref, out_ref, n_scr, *, head_dim, n_norm_tiles,
                 row_chunk):
    j = pl.program_id(1)
    tn = out_ref.shape[1]

    @pl.when(j == 0)
    def _():
        n_scr[...] = _rms(x_ref[...], gin_ref[...]).astype(BF16)

    is_norm = j < n_norm_tiles
    for c in range(n_scr.shape[0] // row_chunk):
        rows = slice(c * row_chunk, (c + 1) * row_chunk)
        acc = _dot(n_scr[rows, :], w_ref[...])
        for h in range(tn // head_dim):
            sl = slice(h * head_dim, (h + 1) * head_dim)
            a = acc[:, sl]
            r = lax.rsqrt(jnp.mean(a * a, axis=-1, keepdims=True) + EPS)
            y = a * jnp.where(is_norm, r, 1.0) * gcol_ref[:, sl]
            out_ref[rows, sl] = y.astype(out_ref.dtype)


def _proj(x, g_in, w, gcol, *, tm, tn, head_dim, norm_cols, out_dtype, name, row_chunk=256):
    t, d = x.shape
    n = w.shape[1]
    pipelined = [((tm, d), F32), ((d, tn), BF16), ((tm, tn), out_dtype)]
    kern = functools.partial(_proj_kernel, head_dim=head_dim, n_norm_tiles=norm_cols // tn,
                             row_chunk=min(row_chunk, tm))
    return pl.pallas_call(
        kern,
        grid=(t // tm, n // tn),
        in_specs=[
            pl.BlockSpec((tm, d), lambda i, j: (i, 0)),
            pl.BlockSpec((1, d), lambda i, j: (0, 0)),
            pl.BlockSpec((d, tn), lambda i, j: (0, j)),
            pl.BlockSpec((1, tn), lambda i, j: (0, j)),
        ],
        out_specs=pl.BlockSpec((tm, tn), lambda i, j: (i, j)),
        out_shape=jax.ShapeDtypeStruct((t, n), out_dtype),
        scratch_shapes=[pltpu.VMEM((tm, d), BF16)],
        compiler_params=pltpu.CompilerParams(
            dimension_semantics=("arbitrary", "arbitrary"),
            vmem_limit_bytes=_vmem_limit(pipelined, [((tm, d), BF16)],
                                         temps=3 * _nbytes((tm, tn), F32) + _nbytes((tm, d), F32))),
        name=name,
    )(x, g_in, w, gcol)


def _proj_resident_kernel(x_ref, gin_ref, w_ref, gcol_ref, out_ref, tail_ref, *, head_dim,
                          norm_cols, col_tile, row_chunk, tail_cols):
    tm = x_ref.shape[0]
    for c in range(tm // row_chunk):
        rows = slice(c * row_chunk, (c + 1) * row_chunk)
        n = _rms(x_ref[rows, :], gin_ref[...]).astype(BF16)
        for ct in range(out_ref.shape[1] // col_tile):
            acc = _dot(n, w_ref[:, ct * col_tile:(ct + 1) * col_tile])
            for h in range(col_tile // head_dim):
                sl = slice(ct * col_tile + h * head_dim, ct * col_tile + (h + 1) * head_dim)
                y = acc[:, h * head_dim:(h + 1) * head_dim]
                if sl.start < norm_cols:
                    y = _rms(y, gcol_ref[:, sl])
                out_ref[rows, sl] = y.astype(out_ref.dtype)
                if tail_cols[0] <= sl.start < tail_cols[1]:
                    tail_ref[rows, sl.start - tail_cols[0]:sl.stop - tail_cols[0]] = y


def _proj_resident(x, g_in, w, gcol, *, tm, seq_len, tail_cols, head_dim, norm_cols, out_dtype,
                   name, col_tile=1024, row_chunk=256):
    t, d = x.shape
    n = w.shape[1]
    tiles_per_seq = seq_len // tm
    n_tail = tail_cols[1] - tail_cols[0]
    const = lambda shape: pl.BlockSpec(shape, lambda i: (0, 0), pipeline_mode=pl.Buffered(1))
    kern = functools.partial(_proj_resident_kernel, head_dim=head_dim, norm_cols=norm_cols,
                             col_tile=col_tile, row_chunk=row_chunk, tail_cols=tail_cols)
    return pl.pallas_call(
        kern,
        grid=(t // tm,),
        in_specs=[pl.BlockSpec((tm, d), lambda i: (i, 0)), const((1, d)), const((d, n)),
                  const((1, n))],
        out_specs=[pl.BlockSpec((tm, n), lambda i: (i, 0)),
                   pl.BlockSpec((tm, n_tail), lambda i: (i // tiles_per_seq, 0))],
        out_shape=[jax.ShapeDtypeStruct((t, n), out_dtype),
                   jax.ShapeDtypeStruct((t // seq_len * tm, n_tail), F32)],
        compiler_params=pltpu.CompilerParams(
            dimension_semantics=("arbitrary",),
            vmem_limit_bytes=_vmem_limit(
                [((tm, d), F32), ((tm, n), out_dtype), ((tm, n_tail), F32)], [((d, n), BF16)],
                temps=_nbytes((row_chunk, d), F32) + 3 * _nbytes((row_chunk, col_tile), F32))),
        name=name,
    )(x, g_in, w, gcol)


def _toeplitz(vec, rows):
    x = jnp.broadcast_to(vec, (rows, KEY_PIECE))
    r = lax.broadcasted_iota(jnp.int32, (rows, KEY_PIECE), 0)
    for b in range((rows - 1).bit_length()):
        x = jnp.where((r >> b) & 1 == 1, pltpu.roll(x, 1 << b, 1), x)
    return x


def _build_band_bias(pq_ref, bias_scr, mask_band, keys_major=False):
    rows = bias_scr.shape[2] if keys_major else bias_scr.shape[1]
    r = lax.broadcasted_iota(jnp.int32, (rows, KEY_PIECE), 0)
    c = lax.broadcasted_iota(jnp.int32, (rows, KEY_PIECE), 1)
    upper = c >= r
    for h in range(N_HEADS):
        top = jnp.broadcast_to(pq_ref[h, 0:1, :], (rows, KEY_PIECE))
        tp = _toeplitz(pq_ref[h, 1:2, :], rows)
        tq = _toeplitz(pq_ref[h, 2:3, :], rows)
        pieces = [top] * (ATTN_PIECES - 2) + [jnp.where(upper, tp, top), jnp.where(upper, tq, tp)]
        for w, piece in enumerate(pieces):
            piece = piece * LOG2E
            if mask_band:
                kc = w * (KEY_PIECE // CHUNK) + c // CHUNK
                qc = r // CHUNK
                piece = jnp.where((kc >= qc) & (kc <= qc + N_PREV_CHUNKS), piece, NEG_INF)
            if keys_major:
                bias_scr[h, w * KEY_PIECE:(w + 1) * KEY_PIECE, :] = piece.T
            else:
                bias_scr[h, :, w * KEY_PIECE:(w + 1) * KEY_PIECE] = piece


def _rel_rows(table):
    h = table.shape[0]
    top = table[:, 2 * REL_CLIP:]
    lo = table[:, :1]
    p = jnp.concatenate([jnp.broadcast_to(top, (h, REL_CLIP + 1)),
                         table[:, 2 * REL_CLIP - 1:REL_CLIP:-1]], axis=1)
    q = jnp.concatenate([table[:, REL_CLIP::-1],
                         jnp.broadcast_to(lo, (h, KEY_PIECE - REL_CLIP - 1))], axis=1)
    return jnp.stack([jnp.broadcast_to(top, (h, KEY_PIECE)), p, q], axis=1).astype(F32)


def _softmax_pv(scores, values):
    widths = {s.shape[1] for s in scores}
    if len(widths) == 1:
        m = jnp.max(functools.reduce(jnp.maximum, scores), axis=-1, keepdims=True)
    else:
        m = functools.reduce(jnp.maximum, [jnp.max(s, axis=-1, keepdims=True) for s in scores])
    ps = [jnp.exp2(s - m) for s in scores]
    if len(widths) == 1:
        l = jnp.sum(functools.reduce(jnp.add, ps), axis=-1, keepdims=True)
    else:
        l = functools.reduce(jnp.add, [jnp.sum(p, axis=-1, keepdims=True) for p in ps])
    o = functools.reduce(jnp.add, [_dot(p.astype(BF16), v) for p, v in zip(ps, values)])
    return o * (1.0 / l)


def _prompt_attn_kernel(pq_ref, q_ref, *rest):
    n_blk = ATTN_PIECES - 1 + ATTN_SUB
    k_refs = rest[:n_blk]
    v_refs = rest[n_blk:2 * n_blk]
    o_ref, bias_scr = rest[2 * n_blk:]
    t = pl.program_id(1)
    qk_scale = HEAD_DIM ** -0.5 * LOG2E

    @pl.when((pl.program_id(0) == 0) & (t == 0))
    def _():
        _build_band_bias(pq_ref, bias_scr, mask_band=True, keys_major=True)

    def attend(seq_start):
        ones = jnp.ones((KEY_PIECE, HEAD_DIM), BF16)
        for u in range(ATTN_SUB):
            first_piece = max(0, ATTN_PIECES - 1 - u) if seq_start else 0
            pieces = range(first_piece, ATTN_PIECES)
            for h in range(N_HEADS):
                sl = slice(h * HEAD_DIM, (h + 1) * HEAD_DIM)
                q = q_ref[0, u * KEY_PIECE:(u + 1) * KEY_PIECE, sl]
                st = [_dot_nt(k_refs[u + w][0, :, sl], q) * qk_scale
                      + bias_scr[h, w * KEY_PIECE:(w + 1) * KEY_PIECE, :] for w in pieces]
                m = jnp.max(functools.reduce(jnp.maximum, st), axis=0, keepdims=True)
                o = None
                for w, s in zip(pieces, st):
                    v_ones = jnp.concatenate([v_refs[u + w][0, :, sl], ones], axis=1)
                    part = _dot_tn(jnp.exp2(s - m).astype(BF16), v_ones)
                    o = part if o is None else o + part
                o_ref[0, u * KEY_PIECE:(u + 1) * KEY_PIECE, sl] = (
                    o[:, :HEAD_DIM] / o[:, HEAD_DIM:]).astype(o_ref.dtype)

    pl.when(t == 0)(functools.partial(attend, True))
    pl.when(t > 0)(functools.partial(attend, False))


def _prompt_attn(proj, pq, *, batch, seq):
    kp = KEY_PIECE
    tq = ATTN_SUB * kp
    n_blk = ATTN_PIECES - 1 + ATTN_SUB

    def kv_spec(p, col):
        return pl.BlockSpec(
            (1, kp, ATTN_WIDTH),
            lambda b, t: (b, jnp.maximum(t * ATTN_SUB - (ATTN_PIECES - 1) + p, 0), col))

    blk = ((1, kp, ATTN_WIDTH), BF16)
    qblk = ((1, tq, ATTN_WIDTH), BF16)
    bias_shape = (N_HEADS, ATTN_PIECES * kp, kp)
    return pl.pallas_call(
        _prompt_attn_kernel,
        grid=(batch, seq // tq),
        in_specs=[pl.BlockSpec(pq.shape, lambda b, t: (0, 0, 0)),
                  pl.BlockSpec((1, tq, ATTN_WIDTH), lambda b, t: (b, t, 0))]
        + [kv_spec(p, 1) for p in range(n_blk)]
        + [kv_spec(p, 2) for p in range(n_blk)],
        out_specs=pl.BlockSpec((1, tq, ATTN_WIDTH), lambda b, t: (b, t, 0)),
        out_shape=jax.ShapeDtypeStruct((batch, seq, ATTN_WIDTH), BF16),
        scratch_shapes=[pltpu.VMEM(bias_shape, F32)],
        compiler_params=pltpu.CompilerParams(
            dimension_semantics=("arbitrary", "arbitrary"),
            vmem_limit_bytes=_vmem_limit([blk] * (2 * n_blk) + [qblk] * 2, [(bias_shape, F32)],
                                         temps=16 * _nbytes((kp, kp), F32))),
        name="prompt_band_attn",
    )(pq, proj, *([proj] * (2 * n_blk)))


def _sample_attn_kernel(pq_ref, q_ref, kn_ref, vn_ref, kf_ref, vf_ref, ck_ref, cv_ref,
                        o_ref, sk_ref, sv_ref, bias_scr):
    qk_scale = HEAD_DIM ** -0.5 * LOG2E
    t_new = q_ref.shape[1]
    n_cache = ck_ref.shape[1] // N_HEADS
    keep = (n_cache - t_new) * N_HEADS

    @pl.when(pl.program_id(0) == 0)
    def _():
        _build_band_bias(pq_ref, bias_scr, mask_band=False)

    for h in range(N_HEADS):
        sl = slice(h * HEAD_DIM, (h + 1) * HEAD_DIM)
        head_rows = pl.ds(h, n_cache, stride=N_HEADS)
        q = q_ref[0, :, sl]
        s_c = _dot_nt(q, ck_ref[0, head_rows, :].astype(BF16)) * qk_scale + bias_scr[h, :, :n_cache]
        s_n = _dot_nt(q, kn_ref[0, :, sl]) * qk_scale + bias_scr[h, :, n_cache:n_cache + t_new]
        o = _softmax_pv([s_c, s_n], [cv_ref[0, head_rows, :].astype(BF16), vn_ref[0, :, sl]])
        o_ref[0, :, sl] = o.astype(o_ref.dtype)
        new_rows = pl.ds(keep + h, t_new, stride=N_HEADS)
        sk_ref[0, new_rows, :] = kf_ref[0, :, sl]
        sv_ref[0, new_rows, :] = vf_ref[0, :, sl]
    sk_ref[0, :keep, :] = ck_ref[0, t_new * N_HEADS:, :]
    sv_ref[0, :keep, :] = cv_ref[0, t_new * N_HEADS:, :]


def _sample_attn(proj, kv_f32, cache_k, cache_v, pq):
    b, t, _ = proj.shape
    lh = cache_k.shape[1]
    assert lh == BAND * N_HEADS and t <= KEY_PIECE
    new_bf = ((1, t, ATTN_WIDTH), BF16)
    new_f = ((1, t, ATTN_WIDTH), F32)
    cache_blk = ((1, lh, HEAD_DIM), F32)
    bias_shape = (N_HEADS, t, ATTN_PIECES * KEY_PIECE)

    def new_spec(col):
        return pl.BlockSpec((1, t, ATTN_WIDTH), lambda i: (i, 0, col))

    cache_spec = pl.BlockSpec((1, lh, HEAD_DIM), lambda i: (i, 0, 0))
    return pl.pallas_call(
        _sample_attn_kernel,
        grid=(b,),
        in_specs=[pl.BlockSpec(pq.shape, lambda i: (0, 0, 0)),
                  new_spec(0), new_spec(1), new_spec(2), new_spec(0), new_spec(1),
                  cache_spec, cache_spec],
        out_specs=[new_spec(0), cache_spec, cache_spec],
        out_shape=[jax.ShapeDtypeStruct((b, t, ATTN_WIDTH), BF16),
                   jax.ShapeDtypeStruct((b, lh, HEAD_DIM), F32),
                   jax.ShapeDtypeStruct((b, lh, HEAD_DIM), F32)],
        scratch_shapes=[pltpu.VMEM(bias_shape, F32)],
        compiler_params=pltpu.CompilerParams(
            dimension_semantics=("arbitrary",),
            vmem_limit_bytes=_vmem_limit(
                [new_bf] * 4 + [new_f] * 2 + [cache_blk] * 4, [(bias_shape, F32)],
                temps=8 * _nbytes((t, BAND), F32) + 2 * _nbytes((BAND, HEAD_DIM), F32))),
        name="sample_band_attn",
    )(pq, proj, proj, proj, kv_f32, kv_f32, cache_k, cache_v)


def _causal_dwconv3(x, prev, w_ref, nseg, lseg):
    c = x.shape[1]
    p0 = prev[:, 0:1, :]
    p1 = prev[:, 1:2, :]
    x3 = x.reshape(nseg, lseg, c)
    sh1 = pltpu.roll(x, 1, 0).reshape(nseg, lseg, c)
    sh2 = pltpu.roll(x, 2, 0).reshape(nseg, lseg, c)
    pos = lax.broadcasted_iota(jnp.int32, (nseg, F32_SUBLANES, c), 1)
    top1 = jnp.where(pos == 0, p1, sh1[:, :F32_SUBLANES])
    top2 = jnp.where(pos == 0, p0, jnp.where(pos == 1, p1, sh2[:, :F32_SUBLANES]))
    sh1 = jnp.concatenate([top1, sh1[:, F32_SUBLANES:]], axis=1)
    sh2 = jnp.concatenate([top2, sh2[:, F32_SUBLANES:]], axis=1)
    y = sh2 * w_ref[0:1, :] + sh1 * w_ref[1:2, :] + x3 * w_ref[2:3, :]
    return y.reshape(nseg * lseg, c), x3[:, lseg - (CONV_K - 1):, :]


def _conv_prev(c, row_chunk, lseg, tiles_per_seq, tile_idx, prev_ref, carry, tail):
    if (c * row_chunk) % lseg != 0:
        return tail[None]
    if tiles_per_seq > 1:
        return jnp.where(tile_idx % tiles_per_seq == 0, prev_ref[...], carry[None])
    s0 = c * row_chunk // lseg
    return prev_ref[s0:s0 + max(1, row_chunk // lseg)]


def _mid_kernel(a_ref, b_ref, c_ref, u_ref, prev_ref, x_ref, mk_ref, mv_ref,
                wconv_ref, ga_ref, gc_ref, wout_ref, g2_ref, wxq_ref, gxq_ref, wxo_ref,
                h_ref, newc_ref, mixed_scr, o_scr, carry_scr, *, lseg, tiles_per_seq, row_chunk):
    i = pl.program_id(0)
    tm = x_ref.shape[0]
    seg_per_chunk = max(1, row_chunk // lseg)
    lseg_c = min(row_chunk, lseg)
    sm_scale = X_HEAD_DIM ** -0.5 * LOG2E
    n_chunks = tm // row_chunk
    state = [dict() for _ in range(n_chunks)]

    def rows_of(c):
        return slice(c * row_chunk, (c + 1) * row_chunk)

    def mix(c):
        rows, st = rows_of(c), state[c]
        s0 = c * row_chunk // lseg
        mixed_scr[rows, :ATTN_WIDTH] = _rms(a_ref[rows, :].astype(F32), ga_ref[...]).astype(BF16)
        cu = c_ref[rows, :].astype(F32) * u_ref[rows, :].astype(F32)
        tail = state[c - 1]["tail"] if c > 0 else None
        prev = _conv_prev(c, row_chunk, lseg, tiles_per_seq, i, prev_ref, carry_scr[...], tail)
        st["tail"] = cu[row_chunk - (CONV_K - 1):, :]
        conv, new_conv = _causal_dwconv3(cu, prev, wconv_ref, seg_per_chunk, lseg_c)
        if ((c + 1) * row_chunk) % lseg == 0:
            newc_ref[s0:s0 + seg_per_chunk] = new_conv
        mixed_scr[rows, ATTN_WIDTH:] = _rms(b_ref[rows, :].astype(F32) * conv,
                                            gc_ref[...]).astype(BF16)

    def out_proj(c):
        rows = rows_of(c)
        h_ref[rows, :] = x_ref[rows, :] + _dot(mixed_scr[rows, :], wout_ref[...])

    def q_proj(c):
        rows, st = rows_of(c), state[c]
        st["qx"] = _dot(_rms(h_ref[rows, :], g2_ref[...]).astype(BF16), wxq_ref[...])

    def cross_attn(c):
        st = state[c]
        s0 = c * row_chunk // lseg
        for hd in range(X_HEADS):
            sl = slice(hd * X_HEAD_DIM, (hd + 1) * X_HEAD_DIM)
            qh = _rms(st["qx"][:, sl], gxq_ref[...]).astype(BF16)
            for s in range(seg_per_chunk):
                srows = slice(s * lseg_c, (s + 1) * lseg_c)
                head = (s0 + s, slice(None), sl) if len(mk_ref.shape) == 3 else (s0 + s, slice(None), hd)
                sc = _dot_nt(qh[srows], mk_ref[head].astype(BF16)) * sm_scale
                o = _softmax_pv([sc], [mv_ref[head].astype(BF16)])
                o_scr[c * row_chunk + s * lseg_c:c * row_chunk + (s + 1) * lseg_c, sl] = o.astype(BF16)

    def x_out(c):
        rows = rows_of(c)
        h_ref[rows, :] += _dot(o_scr[rows, :], wxo_ref[...])

    stages = (mix, out_proj, q_proj, cross_attn, x_out)
    for step in range(n_chunks + len(stages) - 1):
        for k, stage in enumerate(stages):
            c = step - k
            if 0 <= c < n_chunks:
                stage(c)
    if tiles_per_seq > 1:
        carry_scr[...] = state[-1]["tail"]


def _mid(a, proj, conv_prev, x, mem_k, mem_v, w_conv, g_a, g_c, w_out, g2, w_xq, g_xq, w_xo,
         *, tm, seq_len, row_chunk=256):
    t, d = x.shape
    cw = conv_prev.shape[2]
    n_mem = mem_k.shape[1]
    lseg = min(tm, seq_len)
    nseg = tm // lseg
    tps = max(1, seq_len // tm)
    row_chunk = min(row_chunk, tm)
    assert lseg % row_chunk == 0 or row_chunk % lseg == 0

    def col_spec(col):
        return pl.BlockSpec((tm, cw), lambda i: (i, col))

    def const_spec(arr):
        return pl.BlockSpec(arr.shape, lambda i: (0,) * arr.ndim, pipeline_mode=pl.Buffered(1))

    seq_spec3 = lambda rows, width: pl.BlockSpec((nseg, rows, width), lambda i: (i // tps, 0, 0))
    kern = functools.partial(_mid_kernel, lseg=lseg, tiles_per_seq=tps, row_chunk=row_chunk)
    pipelined = ([((tm, cw), BF16)] * 4 + [((nseg, 2, cw), F32)] * 2 + [((tm, d), F32)] * 2
                 + [((nseg, n_mem, X_WIDTH), mem_k.dtype)] * 2)
    mem_spec = pl.BlockSpec((nseg,) + mem_k.shape[1:],
                            lambda i: (i // tps,) + (0,) * (mem_k.ndim - 1))
    resident = [(w.shape, w.dtype) for w in (w_out, w_xq, w_xo)]
    resident += [((tm, d), BF16), ((tm, X_WIDTH), BF16)]
    return pl.pallas_call(
        kern,
        grid=(t // tm,),
        in_specs=[
            pl.BlockSpec((tm, ATTN_WIDTH), lambda i: (i, 0)),
            col_spec(3), col_spec(4), col_spec(5),
            seq_spec3(2, cw),
            pl.BlockSpec((tm, d), lambda i: (i, 0)),
            mem_spec, mem_spec,
            const_spec(w_conv), const_spec(g_a), const_spec(g_c), const_spec(w_out),
            const_spec(g2), const_spec(w_xq), const_spec(g_xq), const_spec(w_xo),
        ],
        out_specs=[pl.BlockSpec((tm, d), lambda i: (i, 0)), seq_spec3(2, cw)],
        out_shape=[jax.ShapeDtypeStruct((t, d), F32),
                   jax.ShapeDtypeStruct(conv_prev.shape, F32)],
        scratch_shapes=[pltpu.VMEM((tm, d), BF16), pltpu.VMEM((tm, X_WIDTH), BF16),
                        pltpu.VMEM((CONV_K - 1, cw), F32)],
        compiler_params=pltpu.CompilerParams(
            dimension_semantics=("arbitrary",),
            vmem_limit_bytes=_vmem_limit(pipelined, resident,
                                         temps=8 * _nbytes((row_chunk, d), F32))),
        name="mid",
    )(a, proj, proj, proj, conv_prev, x, mem_k, mem_v,
      w_conv, g_a, g_c, w_out, g2, w_xq, g_xq, w_xo)


def _ffn_kernel(h_ref, g3_ref, wup_ref, wgate_ref, wconv_ref, wdown_ref, prev_ref,
                y_ref, newf_ref, n3_scr, carry_scr, *, lseg, tiles_per_seq, row_chunk,
                first_row_chunk):
    i = pl.program_id(0)
    f = pl.program_id(1)
    tm = h_ref.shape[0]

    def body(first, row_chunk):
        seg_per_chunk = max(1, row_chunk // lseg)
        lseg_c = min(row_chunk, lseg)
        tail = None
        for c in range(tm // row_chunk):
            rows = slice(c * row_chunk, (c + 1) * row_chunk)
            if first:
                h = h_ref[rows, :]
                n3 = _rms(h, g3_ref[...]).astype(BF16)
                n3_scr[rows, :] = n3
            else:
                n3 = n3_scr[rows, :]
            up = _dot(n3, wup_ref[...])
            gate = _dot(n3, wgate_ref[...])
            prev = _conv_prev(c, row_chunk, lseg, tiles_per_seq, i, prev_ref, carry_scr[f], tail)
            tail = up[row_chunk - (CONV_K - 1):, :]
            conv, new_ffn = _causal_dwconv3(up, prev, wconv_ref, seg_per_chunk, lseg_c)
            if ((c + 1) * row_chunk) % lseg == 0:
                s0 = c * row_chunk // lseg
                newf_ref[f, s0:s0 + seg_per_chunk] = new_ffn
            act = conv / (1.0 + jnp.exp(-conv)) * gate
            down = _dot(act.astype(BF16), wdown_ref[...])
            if first:
                y_ref[rows, :] = h + down
            else:
                y_ref[rows, :] += down
        if tiles_per_seq > 1:
            carry_scr[f] = tail

    pl.when(f == 0)(functools.partial(body, True, first_row_chunk))
    pl.when(f > 0)(functools.partial(body, False, row_chunk))


def _ffn(h, g3, w_up, w_gate, w_conv, w_down, ffn_prev, *, tm, tf, seq_len, row_chunk=512):
    t, d = h.shape
    dff = w_up.shape[1]
    n_seq = ffn_prev.shape[0]
    lseg = min(tm, seq_len)
    nseg = tm // lseg
    tps = max(1, seq_len // tm)
    nf = dff // tf
    row_chunk = min(row_chunk, tm)
    assert lseg % row_chunk == 0 or row_chunk % lseg == 0
    first_row_chunk = min(row_chunk, 256)
    assert lseg % first_row_chunk == 0 or first_row_chunk % lseg == 0
    kern = functools.partial(_ffn_kernel, lseg=lseg, tiles_per_seq=tps, row_chunk=row_chunk,
                             first_row_chunk=first_row_chunk)
    prev_spec = pl.BlockSpec((nseg, CONV_K - 1, tf), lambda i, f: (i // tps, 0, f))
    newf_blk = (nf, nseg, CONV_K - 1, tf)
    pipelined = ([((tm, d), F32)] * 2 + [((d, tf), BF16)] * 3 + [((nseg, 2, tf), F32)]
                 + [(newf_blk, F32), ((CONV_K, tf), F32)])
    resident = [((tm, d), BF16), ((nf, CONV_K - 1, tf), F32)]
    y, new_ffn = pl.pallas_call(
        kern,
        grid=(t // tm, nf),
        in_specs=[
            pl.BlockSpec((tm, d), lambda i, f: (i, 0)),
            pl.BlockSpec((1, d), lambda i, f: (0, 0)),
            pl.BlockSpec((d, tf), lambda i, f: (0, f)),
            pl.BlockSpec((d, tf), lambda i, f: (0, f)),
            pl.BlockSpec((CONV_K, tf), lambda i, f: (0, f)),
            pl.BlockSpec((tf, d), lambda i, f: (f, 0)),
            prev_spec,
        ],
        out_specs=[pl.BlockSpec((tm, d), lambda i, f: (i, 0)),
                   pl.BlockSpec(newf_blk, lambda i, f: (0, i // tps, 0, 0))],
        out_shape=[jax.ShapeDtypeStruct((t, d), F32),
                   jax.ShapeDtypeStruct((nf, n_seq, CONV_K - 1, tf), F32)],
        scratch_shapes=[pltpu.VMEM((tm, d), BF16), pltpu.VMEM((nf, CONV_K - 1, tf), F32)],
        compiler_params=pltpu.CompilerParams(
            dimension_semantics=("arbitrary", "arbitrary"),
            vmem_limit_bytes=_vmem_limit(pipelined, resident, temps=10 * _nbytes((tm, tf), F32))),
        name="ffn",
    )(h, g3, w_up, w_gate, w_conv, w_down, ffn_prev)
    return y, new_ffn.transpose(1, 2, 0, 3).reshape(n_seq, CONV_K - 1, dff)


def _layer(x, *, seq_len, proj_fn, attn_fn, conv_prev, ffn_prev, mem_k, mem_v, w, tm_mid, tm_ffn):
    proj, kv_f32 = proj_fn(x)
    a, new_k, new_v = attn_fn(proj, kv_f32)
    h, new_conv = _mid(a.reshape(-1, ATTN_WIDTH), proj, conv_prev, x, mem_k, mem_v,
                       w["w_conv_mix"], w["g_out_attn"], w["g_out_conv"], w["w_out"],
                       w["g_norm2"], w["w_xq"], w["g_xq"], w["w_xo"], tm=tm_mid, seq_len=seq_len)
    y, new_ffn = _ffn(h, w["g_norm3"], w["w_up"], w["w_gate"], w["w_ffn_conv"], w["w_down"],
                      ffn_prev, tm=tm_ffn, tf=512, seq_len=seq_len)
    return y, new_k, new_v, new_conv, new_ffn


def kernel(x_prompt, x_sample, mem_prompt, cache_attn_k, cache_attn_v, cache_conv, cache_ffn_conv, cache_mem_k, cache_mem_v, g_norm1, w_in, g_q, g_k, rel_bias, w_conv_mix, g_out_attn, g_out_conv, w_out, g_norm2, g_mem_norm, w_xq, w_xkv, g_xq, g_xk, w_xo, g_norm3, w_up, w_gate, w_ffn_conv, w_down):
    bp, sp, d = x_prompt.shape
    bs, ss, _ = x_sample.shape
    depth = w_in.shape[0]
    in_width = w_in.shape[2]
    conv_ch = w_conv_mix.shape[2]
    d_ff = w_up.shape[2]
    n_mem = mem_prompt.shape[1]
    n_cache = cache_attn_k.shape[2]
    keep = min(BAND, sp)

    yp = x_prompt.reshape(bp * sp, d)
    ys = x_sample.reshape(bs * ss, d)
    outs = [[] for _ in range(10)]
    for l in range(depth):
        row = lambda g: g[l][None, :]
        w = dict(
            g_norm1=row(g_norm1), w_in=w_in[l].astype(BF16),
            g_in_cols=jnp.concatenate([jnp.tile(g_q[l], N_HEADS), jnp.tile(g_k[l], N_HEADS),
                                       jnp.ones((in_width - 2 * ATTN_WIDTH,), F32)])[None, :],
            w_conv_mix=w_conv_mix[l], g_out_attn=row(g_out_attn), g_out_conv=row(g_out_conv),
            w_out=w_out[l].astype(BF16), g_norm2=row(g_norm2), w_xq=w_xq[l].astype(BF16),
            g_xq=row(g_xq), w_xo=w_xo[l].astype(BF16), g_norm3=row(g_norm3),
            w_up=w_up[l].astype(BF16), w_gate=w_gate[l].astype(BF16),
            w_ffn_conv=w_ffn_conv[l], w_down=w_down[l].astype(BF16))
        pq = _rel_rows(rel_bias[l])

        g_mem_cols = jnp.concatenate([jnp.tile(g_xk[l], X_HEADS), jnp.ones((X_WIDTH,), F32)])[None, :]
        mem_kv = _proj(mem_prompt.reshape(bp * n_mem, d), row(g_mem_norm), w_xkv[l].astype(BF16),
                       g_mem_cols, tm=bp * n_mem, tn=512, head_dim=X_HEAD_DIM,
                       norm_cols=X_WIDTH, out_dtype=F32, name="mem_kv")
        mem_kv = mem_kv.reshape(bp, n_mem, 2 * X_WIDTH)
        mk_p, mv_p = mem_kv[:, :, :X_WIDTH], mem_kv[:, :, X_WIDTH:]

        proj_args = dict(head_dim=HEAD_DIM, norm_cols=2 * ATTN_WIDTH)
        kv_cols = (ATTN_WIDTH, 3 * ATTN_WIDTH)

        def prompt_proj(x):
            return _proj_resident(x, w["g_norm1"], w["w_in"], w["g_in_cols"], tm=keep, seq_len=sp,
                                  tail_cols=kv_cols, out_dtype=BF16, name="in_proj", **proj_args)

        def prompt_attn(proj, kv_tail):
            a = _prompt_attn(proj.reshape(bp, sp, in_width), pq, batch=bp, seq=sp)
            kv = kv_tail.reshape(bp, keep, 2 * ATTN_WIDTH)
            return a, kv[:, :, :ATTN_WIDTH], kv[:, :, ATTN_WIDTH:]

        yp, pk, pv, pc, pf = _layer(
            yp, seq_len=sp, proj_fn=prompt_proj, attn_fn=prompt_attn,
            conv_prev=jnp.zeros((bp, CONV_K - 1, conv_ch), F32),
            ffn_prev=jnp.zeros((bp, CONV_K - 1, d_ff), F32),
            mem_k=mk_p.astype(BF16), mem_v=mv_p.astype(BF16), w=w, tm_mid=512, tm_ffn=1024)

        ck = cache_attn_k[l].reshape(bs, n_cache * N_HEADS, HEAD_DIM)
        cv = cache_attn_v[l].reshape(bs, n_cache * N_HEADS, HEAD_DIM)

        def sample_attn(proj, kv_f32):
            return _sample_attn(proj.reshape(bs, ss, in_width),
                                kv_f32.reshape(bs, ss, 2 * ATTN_WIDTH), ck, cv, pq)

        def sample_proj(x):
            return _proj_resident(x, w["g_norm1"], w["w_in"], w["g_in_cols"], tm=BAND,
                                  seq_len=BAND, tail_cols=kv_cols, out_dtype=BF16,
                                  name="in_proj", **proj_args)

        ys, sk, sv, sc, sf = _layer(
            ys, seq_len=ss, proj_fn=sample_proj, attn_fn=sample_attn,
            conv_prev=cache_conv[l], ffn_prev=cache_ffn_conv[l],
            mem_k=cache_mem_k[l], mem_v=cache_mem_v[l],
            w=w, tm_mid=256, tm_ffn=512)

        for lst, val in zip(outs, (
                pk.reshape(bp, keep, N_HEADS, HEAD_DIM), pv.reshape(bp, keep, N_HEADS, HEAD_DIM),
                pc, pf,
                mk_p.reshape(bp, n_mem, X_HEADS, X_HEAD_DIM), mv_p.reshape(bp, n_mem, X_HEADS, X_HEAD_DIM),
                sk.reshape(bs, n_cache, N_HEADS, HEAD_DIM), sv.reshape(bs, n_cache, N_HEADS, HEAD_DIM),
                sc, sf)):
            lst.append(val)

    return (yp.reshape(bp, sp, d), ys.reshape(bs, ss, d)) + tuple(jnp.stack(o) for o in outs)
```

```python
import functools
import math

import jax
import jax.numpy as jnp
import numpy as np
from jax import lax
from jax.experimental import pallas as pl
from jax.experimental.pallas import tpu as pltpu

CHUNK = 64
N_PREV_CHUNKS = 8
BAND = N_PREV_CHUNKS * CHUNK
N_HEADS = 8
HEAD_DIM = 128
ATTN_WIDTH = N_HEADS * HEAD_DIM
REL_CLIP = 128
CONV_K = 3
X_HEADS = 4
X_HEAD_DIM = 256
X_WIDTH = X_HEADS * X_HEAD_DIM
EPS = 1e-6
NEG_INF = -1e30
LOG2E = math.log2(math.e)

BF16 = jnp.bfloat16
F32 = jnp.float32

V7X_VMEM_BYTES = 64 * 1024 * 1024
F32_SUBLANES = 8

KEY_PIECE = 2 * REL_CLIP
ATTN_PIECES = BAND // KEY_PIECE + 1
assert KEY_PIECE % CHUNK == 0 and BAND % KEY_PIECE == 0
ATTN_SUB = 4


def _nbytes(shape, dtype):
    return int(np.prod(shape)) * jnp.dtype(dtype).itemsize


def _vmem_limit(pipelined, resident=(), temps=0):
    est = 2 * sum(_nbytes(s, d) for s, d in pipelined)
    est += sum(_nbytes(s, d) for s, d in resident) + temps
    return min(int(est * 1.25) + (2 << 20), V7X_VMEM_BYTES - (6 << 20))


def _rms(xf, g):
    return xf * lax.rsqrt(jnp.mean(xf * xf, axis=-1, keepdims=True) + EPS) * g


def _dot(a, b):
    return jnp.dot(a, b, preferred_element_type=F32)


def _dot_nt(a, b):
    return lax.dot_general(a, b, (((1,), (1,)), ((), ())), preferred_element_type=F32)


def _dot_tn(a, b):
    return lax.dot_general(a, b, (((0,), (0,)), ((), ())), preferred_element_type=F32)


def _proj_kernel(x_ref, gin_ref, w_ref, gcol_ref, out_ref, n_scr, *, head_dim, n_norm_tiles,
                 row_chunk):
    j = pl.program_id(1)
    tn = out_ref.shape[1]

    @pl.when(j == 0)
    def _():
        n_scr[...] = _rms(x_ref[...], gin_ref[...]).astype(BF16)

    is_norm = j < n_norm_tiles
    for c in range(n_scr.shape[0] // row_chunk):
        rows = slice(c * row_chunk, (c + 1) * row_chunk)
        acc = _dot(n_scr[rows, :], w_ref[...])
        for h in range(tn // head_dim):
            sl = slice(h * head_dim, (h + 1) * head_dim)
            a = acc[:, sl]
            r = lax.rsqrt(jnp.mean(a * a, axis=-1, keepdims=True) + EPS)
            y = a * jnp.where(is_norm, r, 1.0) * gcol_ref[:, sl]
            out_ref[rows, sl] = y.astype(out_ref.dtype)


def _proj(x, g_in, w, gcol, *, tm, tn, head_dim, norm_cols, out_dtype, name, row_chunk=256):
    t, d = x.shape
    n = w.shape[1]
    pipelined = [((tm, d), F32), ((d, tn), BF16), ((tm, tn), out_dtype)]
    kern = functools.partial(_proj_kernel, head_dim=head_dim, n_norm_tiles=norm_cols // tn,
                             row_chunk=min(row_chunk, tm))
    return pl.pallas_call(
        kern,
        grid=(t // tm, n // tn),
        in_specs=[
            pl.BlockSpec((tm, d), lambda i, j: (i, 0)),
            pl.BlockSpec((1, d), lambda i, j: (0, 0)),
            pl.BlockSpec((d, tn), lambda i, j: (0, j)),
            pl.BlockSpec((1, tn), lambda i, j: (0, j)),
        ],
        out_specs=pl.BlockSpec((tm, tn), lambda i, j: (i, j)),
        out_shape=jax.ShapeDtypeStruct((t, n), out_dtype),
        scratch_shapes=[pltpu.VMEM((tm, d), BF16)],
        compiler_params=pltpu.CompilerParams(
            dimension_semantics=("arbitrary", "arbitrary"),
            vmem_limit_bytes=_vmem_limit(pipelined, [((tm, d), BF16)],
                                         temps=3 * _nbytes((tm, tn), F32) + _nbytes((tm, d), F32))),
        name=name,
    )(x, g_in, w, gcol)


def _proj_resident_kernel(x_ref, gin_ref, w_ref, gcol_ref, out_ref, tail_ref, *, head_dim,
                          norm_cols, col_tile, row_chunk, tail_cols):
    tm = x_ref.shape[0]
    for c in range(tm // row_chunk):
        rows = slice(c * row_chunk, (c + 1) * row_chunk)
        n = _rms(x_ref[rows, :], gin_ref[...]).astype(BF16)
        for ct in range(out_ref.shape[1] // col_tile):
            acc = _dot(n, w_ref[:, ct * col_tile:(ct + 1) * col_tile])
            for h in range(col_tile // head_dim):
                sl = slice(ct * col_tile + h * head_dim, ct * col_tile + (h + 1) * head_dim)
                y = acc[:, h * head_dim:(h + 1) * head_dim]
                if sl.start < norm_cols:
                    y = _rms(y, gcol_ref[:, sl])
                out_ref[rows, sl] = y.astype(out_ref.dtype)
                if tail_cols[0] <= sl.start < tail_cols[1]:
                    part, off = divmod(sl.start - tail_cols[0], tail_ref.shape[2])
                    tail_ref[part, rows, off:off + head_dim] = y


def _proj_resident(x, g_in, w, gcol, *, tm, seq_len, tail_cols, tail_parts, head_dim, norm_cols,
                   out_dtype, name, col_tile=1024, row_chunk=256):
    t, d = x.shape
    n = w.shape[1]
    tiles_per_seq = seq_len // tm
    n_tail = (tail_cols[1] - tail_cols[0]) // tail_parts
    tail_blk = (tail_parts, tm, n_tail)
    const = lambda shape: pl.BlockSpec(shape, lambda i: (0, 0), pipeline_mode=pl.Buffered(1))
    kern = functools.partial(_proj_resident_kernel, head_dim=head_dim, norm_cols=norm_cols,
                             col_tile=col_tile, row_chunk=row_chunk, tail_cols=tail_cols)
    return pl.pallas_call(
        kern,
        grid=(t // tm,),
        in_specs=[pl.BlockSpec((tm, d), lambda i: (i, 0)), const((1, d)), const((d, n)),
                  const((1, n))],
        out_specs=[pl.BlockSpec((tm, n), lambda i: (i, 0)),
                   pl.BlockSpec(tail_blk, lambda i: (0, i // tiles_per_seq, 0))],
        out_shape=[jax.ShapeDtypeStruct((t, n), out_dtype),
                   jax.ShapeDtypeStruct((tail_parts, t // seq_len * tm, n_tail), F32)],
        compiler_params=pltpu.CompilerParams(
            dimension_semantics=("arbitrary",),
            vmem_limit_bytes=_vmem_limit(
                [((tm, d), F32), ((tm, n), out_dtype), (tail_blk, F32)], [((d, n), BF16)],
                temps=_nbytes((row_chunk, d), F32) + 3 * _nbytes((row_chunk, col_tile), F32))),
        name=name,
    )(x, g_in, w, gcol)


def _toeplitz(vec, rows):
    x = jnp.broadcast_to(vec, (rows, KEY_PIECE))
    r = lax.broadcasted_iota(jnp.int32, (rows, KEY_PIECE), 0)
    for b in range((rows - 1).bit_length()):
        x = jnp.where((r >> b) & 1 == 1, pltpu.roll(x, 1 << b, 1), x)
    return x


def _build_band_bias(pq_ref, bias_scr, mask_band, keys_major=False):
    rows = bias_scr.shape[2] if keys_major else bias_scr.shape[1]
    r = lax.broadcasted_iota(jnp.int32, (rows, KEY_PIECE), 0)
    c = lax.broadcasted_iota(jnp.int32, (rows, KEY_PIECE), 1)
    upper = c >= r
    for h in range(N_HEADS):
        top = jnp.broadcast_to(pq_ref[h, 0:1, :], (rows, KEY_PIECE))
        tp = _toeplitz(pq_ref[h, 1:2, :], rows)
        tq = _toeplitz(pq_ref[h, 2:3, :], rows)
        pieces = [top] * (ATTN_PIECES - 2) + [jnp.where(upper, tp, top), jnp.where(upper, tq, tp)]
        for w, piece in enumerate(pieces):
            piece = piece * LOG2E
            if mask_band:
                kc = w * (KEY_PIECE // CHUNK) + c // CHUNK
                qc = r // CHUNK
                piece = jnp.where((kc >= qc) & (kc <= qc + N_PREV_CHUNKS), piece, NEG_INF)
            if keys_major:
                bias_scr[h, w * KEY_PIECE:(w + 1) * KEY_PIECE, :] = piece.T
            else:
                bias_scr[h, :, w * KEY_PIECE:(w + 1) * KEY_PIECE] = piece


def _rel_rows(table):
    h = table.shape[0]
    top = table[:, 2 * REL_CLIP:]
    lo = table[:, :1]
    p = jnp.concatenate([jnp.broadcast_to(top, (h, REL_CLIP + 1)),
                         table[:, 2 * REL_CLIP - 1:REL_CLIP:-1]], axis=1)
    q = jnp.concatenate([table[:, REL_CLIP::-1],
                         jnp.broadcast_to(lo, (h, KEY_PIECE - REL_CLIP - 1))], axis=1)
    return jnp.stack([jnp.broadcast_to(top, (h, KEY_PIECE)), p, q], axis=1).astype(F32)


def _softmax_pv(scores, values):
    widths = {s.shape[1] for s in scores}
    if len(widths) == 1:
        m = jnp.max(functools.reduce(jnp.maximum, scores), axis=-1, keepdims=True)
    else:
        m = functools.reduce(jnp.maximum, [jnp.max(s, axis=-1, keepdims=True) for s in scores])
    ps = [jnp.exp2(s - m) for s in scores]
    if len(widths) == 1:
        l = jnp.sum(functools.reduce(jnp.add, ps), axis=-1, keepdims=True)
    else:
        l = functools.reduce(jnp.add, [jnp.sum(p, axis=-1, keepdims=True) for p in ps])
    o = functools.reduce(jnp.add, [_dot(p.astype(BF16), v) for p, v in zip(ps, values)])
    return o * (1.0 / l)


def _prompt_attn_kernel(pq_ref, q_ref, *rest):
    n_blk = ATTN_PIECES - 1 + ATTN_SUB
    k_refs = rest[:n_blk]
    v_refs = rest[n_blk:2 * n_blk]
    o_ref, bias_scr = rest[2 * n_blk:]
    t = pl.program_id(1)
    qk_scale = HEAD_DIM ** -0.5 * LOG2E

    @pl.when((pl.program_id(0) == 0) & (t == 0))
    def _():
        _build_band_bias(pq_ref, bias_scr, mask_band=True, keys_major=True)

    def attend(seq_start):
        ones = jnp.ones((KEY_PIECE, HEAD_DIM), BF16)
        for u in range(ATTN_SUB):
            first_piece = max(0, ATTN_PIECES - 1 - u) if seq_start else 0
            pieces = range(first_piece, ATTN_PIECES)
            for h in range(N_HEADS):
                sl = slice(h * HEAD_DIM, (h + 1) * HEAD_DIM)
                q = q_ref[0, u * KEY_PIECE:(u + 1) * KEY_PIECE, sl]
                st = [_dot_nt(k_refs[u + w][0, :, sl], q) * qk_scale
                      + bias_scr[h, w * KEY_PIECE:(w + 1) * KEY_PIECE, :] for w in pieces]
                m = jnp.max(functools.reduce(jnp.maximum, st), axis=0, keepdims=True)
                o = None
                for w, s in zip(pieces, st):
                    v_ones = jnp.concatenate([v_refs[u + w][0, :, sl], ones], axis=1)
                    part = _dot_tn(jnp.exp2(s - m).astype(BF16), v_ones)
                    o = part if o is None else o + part
                o_ref[0, u * KEY_PIECE:(u + 1) * KEY_PIECE, sl] = (
                    o[:, :HEAD_DIM] / o[:, HEAD_DIM:]).astype(o_ref.dtype)

    pl.when(t == 0)(functools.partial(attend, True))
    pl.when(t > 0)(functools.partial(attend, False))


def _prompt_attn(proj, pq, *, batch, seq):
    kp = KEY_PIECE
    tq = ATTN_SUB * kp
    n_blk = ATTN_PIECES - 1 + ATTN_SUB

    def kv_spec(p, col):
        return pl.BlockSpec(
            (1, kp, ATTN_WIDTH),
            lambda b, t: (b, jnp.maximum(t * ATTN_SUB - (ATTN_PIECES - 1) + p, 0), col))

    blk = ((1, kp, ATTN_WIDTH), BF16)
    qblk = ((1, tq, ATTN_WIDTH), BF16)
    bias_shape = (N_HEADS, ATTN_PIECES * kp, kp)
    return pl.pallas_call(
        _prompt_attn_kernel,
        grid=(batch, seq // tq),
        in_specs=[pl.BlockSpec(pq.shape, lambda b, t: (0, 0, 0)),
                  pl.BlockSpec((1, tq, ATTN_WIDTH), lambda b, t: (b, t, 0))]
        + [kv_spec(p, 1) for p in range(n_blk)]
        + [kv_spec(p, 2) for p in range(n_blk)],
        out_specs=pl.BlockSpec((1, tq, ATTN_WIDTH), lambda b, t: (b, t, 0)),
        out_shape=jax.ShapeDtypeStruct((batch, seq, ATTN_WIDTH), BF16),
        scratch_shapes=[pltpu.VMEM(bias_shape, F32)],
        compiler_params=pltpu.CompilerParams(
            dimension_semantics=("arbitrary", "arbitrary"),
            vmem_limit_bytes=_vmem_limit([blk] * (2 * n_blk) + [qblk] * 2, [(bias_shape, F32)],
                                         temps=16 * _nbytes((kp, kp), F32))),
        name="prompt_band_attn",
    )(pq, proj, *([proj] * (2 * n_blk)))


def _sample_attn_kernel(pq_ref, q_ref, kn_ref, vn_ref, kf_ref, vf_ref, ck_ref, cv_ref,
                        o_ref, sk_ref, sv_ref, bias_scr):
    qk_scale = HEAD_DIM ** -0.5 * LOG2E
    t_new = q_ref.shape[1]
    n_cache = ck_ref.shape[1] // N_HEADS
    keep = (n_cache - t_new) * N_HEADS

    @pl.when(pl.program_id(0) == 0)
    def _():
        _build_band_bias(pq_ref, bias_scr, mask_band=False)

    for h in range(N_HEADS):
        sl = slice(h * HEAD_DIM, (h + 1) * HEAD_DIM)
        head_rows = pl.ds(h, n_cache, stride=N_HEADS)
        q = q_ref[0, :, sl]
        s_c = _dot_nt(q, ck_ref[0, head_rows, :].astype(BF16)) * qk_scale + bias_scr[h, :, :n_cache]
        s_n = _dot_nt(q, kn_ref[0, :, sl]) * qk_scale + bias_scr[h, :, n_cache:n_cache + t_new]
        o = _softmax_pv([s_c, s_n], [cv_ref[0, head_rows, :].astype(BF16), vn_ref[0, :, sl]])
        o_ref[0, :, sl] = o.astype(o_ref.dtype)
        new_rows = pl.ds(keep + h, t_new, stride=N_HEADS)
        sk_ref[0, new_rows, :] = kf_ref[0, :, sl]
        sv_ref[0, new_rows, :] = vf_ref[0, :, sl]
    sk_ref[0, :keep, :] = ck_ref[0, t_new * N_HEADS:, :]
    sv_ref[0, :keep, :] = cv_ref[0, t_new * N_HEADS:, :]


def _sample_attn(proj, k_f32, v_f32, cache_k, cache_v, pq):
    b, t, _ = proj.shape
    lh = cache_k.shape[1]
    assert lh == BAND * N_HEADS and t <= KEY_PIECE
    new_bf = ((1, t, ATTN_WIDTH), BF16)
    new_f = ((1, t, ATTN_WIDTH), F32)
    cache_blk = ((1, lh, HEAD_DIM), F32)
    bias_shape = (N_HEADS, t, ATTN_PIECES * KEY_PIECE)

    def new_spec(col):
        return pl.BlockSpec((1, t, ATTN_WIDTH), lambda i: (i, 0, col))

    cache_spec = pl.BlockSpec((1, lh, HEAD_DIM), lambda i: (i, 0, 0))
    return pl.pallas_call(
        _sample_attn_kernel,
        grid=(b,),
        in_specs=[pl.BlockSpec(pq.shape, lambda i: (0, 0, 0)),
                  new_spec(0), new_spec(1), new_spec(2), new_spec(0), new_spec(0),
                  cache_spec, cache_spec],
        out_specs=[new_spec(0), cache_spec, cache_spec],
        out_shape=[jax.ShapeDtypeStruct((b, t, ATTN_WIDTH), BF16),
                   jax.ShapeDtypeStruct((b, lh, HEAD_DIM), F32),
                   jax.ShapeDtypeStruct((b, lh, HEAD_DIM), F32)],
        scratch_shapes=[pltpu.VMEM(bias_shape, F32)],
        compiler_params=pltpu.CompilerParams(
            dimension_semantics=("arbitrary",),
            vmem_limit_bytes=_vmem_limit(
                [new_bf] * 4 + [new_f] * 2 + [cache_blk] * 4, [(bias_shape, F32)],
                temps=8 * _nbytes((t, BAND), F32) + 2 * _nbytes((BAND, HEAD_DIM), F32))),
        name="sample_band_attn",
    )(pq, proj, proj, proj, k_f32, v_f32, cache_k, cache_v)


def _causal_dwconv3(x, prev, w_ref, nseg, lseg):
    c = x.shape[1]
    p0 = prev[:, 0:1, :]
    p1 = prev[:, 1:2, :]
    x3 = x.reshape(nseg, lseg, c)
    sh1 = pltpu.roll(x, 1, 0).reshape(nseg, lseg, c)
    sh2 = pltpu.roll(x, 2, 0).reshape(nseg, lseg, c)
    pos = lax.broadcasted_iota(jnp.int32, (nseg, F32_SUBLANES, c), 1)
    top1 = jnp.where(pos == 0, p1, sh1[:, :F32_SUBLANES])
    top2 = jnp.where(pos == 0, p0, jnp.where(pos == 1, p1, sh2[:, :F32_SUBLANES]))
    sh1 = jnp.concatenate([top1, sh1[:, F32_SUBLANES:]], axis=1)
    sh2 = jnp.concatenate([top2, sh2[:, F32_SUBLANES:]], axis=1)
    y = sh2 * w_ref[0:1, :] + sh1 * w_ref[1:2, :] + x3 * w_ref[2:3, :]
    return y.reshape(nseg * lseg, c), x3[:, lseg - (CONV_K - 1):, :]


def _conv_prev(c, row_chunk, lseg, tiles_per_seq, tile_idx, prev_ref, carry, tail):
    if (c * row_chunk) % lseg != 0:
        return tail[None]
    if tiles_per_seq > 1:
        return jnp.where(tile_idx % tiles_per_seq == 0, prev_ref[...], carry[None])
    s0 = c * row_chunk // lseg
    return prev_ref[s0:s0 + max(1, row_chunk // lseg)]


def _mid_kernel(a_ref, b_ref, c_ref, u_ref, prev_ref, x_ref, mk_ref, mv_ref,
                wconv_ref, ga_ref, gc_ref, wout_ref, g2_ref, wxq_ref, gxq_ref, wxo_ref,
                h_ref, newc_ref, mixed_scr, o_scr, carry_scr, *, lseg, tiles_per_seq, row_chunk):
    i = pl.program_id(0)
    tm = x_ref.shape[0]
    seg_per_chunk = max(1, row_chunk // lseg)
    lseg_c = min(row_chunk, lseg)
    sm_scale = X_HEAD_DIM ** -0.5 * LOG2E
    n_chunks = tm // row_chunk
    state = [dict() for _ in range(n_chunks)]

    def rows_of(c):
        return slice(c * row_chunk, (c + 1) * row_chunk)

    def mix(c):
        rows, st = rows_of(c), state[c]
        s0 = c * row_chunk // lseg
        mixed_scr[rows, :ATTN_WIDTH] = _rms(a_ref[rows, :].astype(F32), ga_ref[...]).astype(BF16)
        cu = c_ref[rows, :].astype(F32) * u_ref[rows, :].astype(F32)
        tail = state[c - 1]["tail"] if c > 0 else None
        prev = _conv_prev(c, row_chunk, lseg, tiles_per_seq, i, prev_ref, carry_scr[...], tail)
        st["tail"] = cu[row_chunk - (CONV_K - 1):, :]
        conv, new_conv = _causal_dwconv3(cu, prev, wconv_ref, seg_per_chunk, lseg_c)
        if ((c + 1) * row_chunk) % lseg == 0:
            newc_ref[s0:s0 + seg_per_chunk] = new_conv
        mixed_scr[rows, ATTN_WIDTH:] = _rms(b_ref[rows, :].astype(F32) * conv,
                                            gc_ref[...]).astype(BF16)

    def out_proj(c):
        rows = rows_of(c)
        h_ref[rows, :] = x_ref[rows, :] + _dot(mixed_scr[rows, :], wout_ref[...])

    def q_proj(c):
        rows, st = rows_of(c), state[c]
        st["qx"] = _dot(_rms(h_ref[rows, :], g2_ref[...]).astype(BF16), wxq_ref[...])

    def cross_attn(c):
        st = state[c]
        s0 = c * row_chunk // lseg
        for hd in range(X_HEADS):
            sl = slice(hd * X_HEAD_DIM, (hd + 1) * X_HEAD_DIM)
            qh = _rms(st["qx"][:, sl], gxq_ref[...]).astype(BF16)
            for s in range(seg_per_chunk):
                srows = slice(s * lseg_c, (s + 1) * lseg_c)
                head = (s0 + s, slice(None), sl) if len(mk_ref.shape) == 3 else (s0 + s, slice(None), hd)
                sc = _dot_nt(qh[srows], mk_ref[head].astype(BF16)) * sm_scale
                o = _softmax_pv([sc], [mv_ref[head].astype(BF16)])
                o_scr[c * row_chunk + s * lseg_c:c * row_chunk + (s + 1) * lseg_c, sl] = o.astype(BF16)

    def x_out(c):
        rows = rows_of(c)
        h_ref[rows, :] += _dot(o_scr[rows, :], wxo_ref[...])

    stages = (mix, out_proj, q_proj, cross_attn, x_out)
    for step in range(n_chunks + len(stages) - 1):
        for k, stage in enumerate(stages):
            c = step - k
            if 0 <= c < n_chunks:
                stage(c)
    if tiles_per_seq > 1:
        carry_scr[...] = state[-1]["tail"]


def _mid(a, proj, conv_prev, x, mem_k, mem_v, w_conv, g_a, g_c, w_out, g2, w_xq, g_xq, w_xo,
         *, tm, seq_len, row_chunk=256):
    t, d = x.shape
    cw = conv_prev.shape[2]
    n_mem = mem_k.shape[1]
    lseg = min(tm, seq_len)
    nseg = tm // lseg
    tps = max(1, seq_len // tm)
    row_chunk = min(row_chunk, tm)
    assert lseg % row_chunk == 0 or row_chunk % lseg == 0

    def col_spec(col):
        return pl.BlockSpec((tm, cw), lambda i: (i, col))

    def const_spec(arr):
        return pl.BlockSpec(arr.shape, lambda i: (0,) * arr.ndim, pipeline_mode=pl.Buffered(1))

    seq_spec3 = lambda rows, width: pl.BlockSpec((nseg, rows, width), lambda i: (i // tps, 0, 0))
    kern = functools.partial(_mid_kernel, lseg=lseg, tiles_per_seq=tps, row_chunk=row_chunk)
    pipelined = ([((tm, cw), BF16)] * 4 + [((nseg, 2, cw), F32)] * 2 + [((tm, d), F32)] * 2
                 + [((nseg, n_mem, X_WIDTH), mem_k.dtype)] * 2)
    mem_spec = pl.BlockSpec((nseg,) + mem_k.shape[1:],
                            lambda i: (i // tps,) + (0,) * (mem_k.ndim - 1))
    resident = [(w.shape, w.dtype) for w in (w_out, w_xq, w_xo)]
    resident += [((tm, d), BF16), ((tm, X_WIDTH), BF16)]
    return pl.pallas_call(
        kern,
        grid=(t // tm,),
        in_specs=[
            pl.BlockSpec((tm, ATTN_WIDTH), lambda i: (i, 0)),
            col_spec(3), col_spec(4), col_spec(5),
            seq_spec3(2, cw),
            pl.BlockSpec((tm, d), lambda i: (i, 0)),
            mem_spec, mem_spec,
            const_spec(w_conv), const_spec(g_a), const_spec(g_c), const_spec(w_out),
            const_spec(g2), const_spec(w_xq), const_spec(g_xq), const_spec(w_xo),
        ],
        out_specs=[pl.BlockSpec((tm, d), lambda i: (i, 0)), seq_spec3(2, cw)],
        out_shape=[jax.ShapeDtypeStruct((t, d), F32),
                   jax.ShapeDtypeStruct(conv_prev.shape, F32)],
        scratch_shapes=[pltpu.VMEM((tm, d), BF16), pltpu.VMEM((tm, X_WIDTH), BF16),
                        pltpu.VMEM((CONV_K - 1, cw), F32)],
        compiler_params=pltpu.CompilerParams(
            dimension_semantics=("arbitrary",),
            vmem_limit_bytes=_vmem_limit(pipelined, resident,
                                         temps=8 * _nbytes((row_chunk, d), F32))),
        name="mid",
    )(a, proj, proj, proj, conv_prev, x, mem_k, mem_v,
      w_conv, g_a, g_c, w_out, g2, w_xq, g_xq, w_xo)


def _ffn_kernel(h_ref, g3_ref, wup_ref, wgate_ref, wconv_ref, wdown_ref, prev_ref,
                y_ref, newf_ref, n3_scr, carry_scr, *, lseg, tiles_per_seq, row_chunk,
                first_row_chunk):
    i = pl.program_id(0)
    f = pl.program_id(1)
    tm = h_ref.shape[0]

    def body(first, row_chunk):
        seg_per_chunk = max(1, row_chunk // lseg)
        lseg_c = min(row_chunk, lseg)
        tail = None
        for c in range(tm // row_chunk):
            rows = slice(c * row_chunk, (c + 1) * row_chunk)
            if first:
                h = h_ref[rows, :]
                n3 = _rms(h, g3_ref[...]).astype(BF16)
                n3_scr[rows, :] = n3
            else:
                n3 = n3_scr[rows, :]
            up = _dot(n3, wup_ref[...])
            gate = _dot(n3, wgate_ref[...])
            prev = _conv_prev(c, row_chunk, lseg, tiles_per_seq, i, prev_ref, carry_scr[f], tail)
            tail = up[row_chunk - (CONV_K - 1):, :]
            conv, new_ffn = _causal_dwconv3(up, prev, wconv_ref, seg_per_chunk, lseg_c)
            if ((c + 1) * row_chunk) % lseg == 0:
                s0 = c * row_chunk // lseg
                newf_ref[f, s0:s0 + seg_per_chunk] = new_ffn
            act = conv / (1.0 + jnp.exp(-conv)) * gate
            down = _dot(act.astype(BF16), wdown_ref[...])
            if first:
                y_ref[rows, :] = h + down
            else:
                y_ref[rows, :] += down
        if tiles_per_seq > 1:
            carry_scr[f] = tail

    pl.when(f == 0)(functools.partial(body, True, first_row_chunk))
    pl.when(f > 0)(functools.partial(body, False, row_chunk))


def _ffn(h, g3, w_up, w_gate, w_conv, w_down, ffn_prev, *, tm, tf, seq_len, row_chunk=512):
    t, d = h.shape
    dff = w_up.shape[1]
    n_seq = ffn_prev.shape[0]
    lseg = min(tm, seq_len)
    nseg = tm // lseg
    tps = max(1, seq_len // tm)
    nf = dff // tf
    row_chunk = min(row_chunk, tm)
    assert lseg % row_chunk == 0 or row_chunk % lseg == 0
    first_row_chunk = min(row_chunk, 256)
    assert lseg % first_row_chunk == 0 or first_row_chunk % lseg == 0
    kern = functools.partial(_ffn_kernel, lseg=lseg, tiles_per_seq=tps, row_chunk=row_chunk,
                             first_row_chunk=first_row_chunk)
    prev_spec = pl.BlockSpec((nseg, CONV_K - 1, tf), lambda i, f: (i // tps, 0, f))
    newf_blk = (nf, nseg, CONV_K - 1, tf)
    pipelined = ([((tm, d), F32)] * 2 + [((d, tf), BF16)] * 3 + [((nseg, 2, tf), F32)]
                 + [(newf_blk, F32), ((CONV_K, tf), F32)])
    resident = [((tm, d), BF16), ((nf, CONV_K - 1, tf), F32)]
    y, new_ffn = pl.pallas_call(
        kern,
        grid=(t // tm, nf),
        in_specs=[
            pl.BlockSpec((tm, d), lambda i, f: (i, 0)),
            pl.BlockSpec((1, d), lambda i, f: (0, 0)),
            pl.BlockSpec((d, tf), lambda i, f: (0, f)),
            pl.BlockSpec((d, tf), lambda i, f: (0, f)),
            pl.BlockSpec((CONV_K, tf), lambda i, f: (0, f)),
            pl.BlockSpec((tf, d), lambda i, f: (f, 0)),
            prev_spec,
        ],
        out_specs=[pl.BlockSpec((tm, d), lambda i, f: (i, 0)),
                   pl.BlockSpec(newf_blk, lambda i, f: (0, i // tps, 0, 0))],
        out_shape=[jax.ShapeDtypeStruct((t, d), F32),
                   jax.ShapeDtypeStruct((nf, n_seq, CONV_K - 1, tf), F32)],
        scratch_shapes=[pltpu.VMEM((tm, d), BF16), pltpu.VMEM((nf, CONV_K - 1, tf), F32)],
        compiler_params=pltpu.CompilerParams(
            dimension_semantics=("arbitrary", "arbitrary"),
            vmem_limit_bytes=_vmem_limit(pipelined, resident, temps=10 * _nbytes((tm, tf), F32))),
        name="ffn",
    )(h, g3, w_up, w_gate, w_conv, w_down, ffn_prev)
    return y, new_ffn.transpose(1, 2, 0, 3).reshape(n_seq, CONV_K - 1, dff)


def _layer(x, *, seq_len, proj_fn, attn_fn, conv_prev, ffn_prev, mem_k, mem_v, w, tm_mid, tm_ffn):
    proj, kv_f32 = proj_fn(x)
    a, new_k, new_v = attn_fn(proj, kv_f32)
    h, new_conv = _mid(a.reshape(-1, ATTN_WIDTH), proj, conv_prev, x, mem_k, mem_v,
                       w["w_conv_mix"], w["g_out_attn"], w["g_out_conv"], w["w_out"],
                       w["g_norm2"], w["w_xq"], w["g_xq"], w["w_xo"], tm=tm_mid, seq_len=seq_len)
    y, new_ffn = _ffn(h, w["g_norm3"], w["w_up"], w["w_gate"], w["w_ffn_conv"], w["w_down"],
                      ffn_prev, tm=tm_ffn, tf=512, seq_len=seq_len)
    return y, new_k, new_v, new_conv, new_ffn


def kernel(x_prompt, x_sample, mem_prompt, cache_attn_k, cache_attn_v, cache_conv, cache_ffn_conv, cache_mem_k, cache_mem_v, g_norm1, w_in, g_q, g_k, rel_bias, w_conv_mix, g_out_attn, g_out_conv, w_out, g_norm2, g_mem_norm, w_xq, w_xkv, g_xq, g_xk, w_xo, g_norm3, w_up, w_gate, w_ffn_conv, w_down):
    bp, sp, d = x_prompt.shape
    bs, ss, _ = x_sample.shape
    depth = w_in.shape[0]
    in_width = w_in.shape[2]
    conv_ch = w_conv_mix.shape[2]
    d_ff = w_up.shape[2]
    n_mem = mem_prompt.shape[1]
    n_cache = cache_attn_k.shape[2]
    keep = min(BAND, sp)

    yp = x_prompt.reshape(bp * sp, d)
    ys = x_sample.reshape(bs * ss, d)
    outs = [[] for _ in range(10)]
    for l in range(depth):
        row = lambda g: g[l][None, :]
        w = dict(
            g_norm1=row(g_norm1), w_in=w_in[l].astype(BF16),
            g_in_cols=jnp.concatenate([jnp.tile(g_q[l], N_HEADS), jnp.tile(g_k[l], N_HEADS),
                                       jnp.ones((in_width - 2 * ATTN_WIDTH,), F32)])[None, :],
            w_conv_mix=w_conv_mix[l], g_out_attn=row(g_out_attn), g_out_conv=row(g_out_conv),
            w_out=w_out[l].astype(BF16), g_norm2=row(g_norm2), w_xq=w_xq[l].astype(BF16),
            g_xq=row(g_xq), w_xo=w_xo[l].astype(BF16), g_norm3=row(g_norm3),
            w_up=w_up[l].astype(BF16), w_gate=w_gate[l].astype(BF16),
            w_ffn_conv=w_ffn_conv[l], w_down=w_down[l].astype(BF16))
        pq = _rel_rows(rel_bias[l])

        g_mem_cols = jnp.concatenate([jnp.tile(g_xk[l], X_HEADS), jnp.ones((X_WIDTH,), F32)])[None, :]
        mem_kv = _proj(mem_prompt.reshape(bp * n_mem, d), row(g_mem_norm), w_xkv[l].astype(BF16),
                       g_mem_cols, tm=bp * n_mem, tn=512, head_dim=X_HEAD_DIM,
                       norm_cols=X_WIDTH, out_dtype=F32, name="mem_kv")
        mem_kv = mem_kv.reshape(bp, n_mem, 2 * X_WIDTH)
        mk_p, mv_p = mem_kv[:, :, :X_WIDTH], mem_kv[:, :, X_WIDTH:]

        proj_args = dict(head_dim=HEAD_DIM, norm_cols=2 * ATTN_WIDTH, out_dtype=BF16,
                         tail_cols=(ATTN_WIDTH, 3 * ATTN_WIDTH), tail_parts=2, name="in_proj")

        def prompt_proj(x):
            return _proj_resident(x, w["g_norm1"], w["w_in"], w["g_in_cols"], tm=keep, seq_len=sp,
                                  **proj_args)

        def prompt_attn(proj, kv_tail):
            a = _prompt_attn(proj.reshape(bp, sp, in_width), pq, batch=bp, seq=sp)
            return a, kv_tail[0].reshape(bp, keep, ATTN_WIDTH), kv_tail[1].reshape(bp, keep, ATTN_WIDTH)

        yp, pk, pv, pc, pf = _layer(
            yp, seq_len=sp, proj_fn=prompt_proj, attn_fn=prompt_attn,
            conv_prev=jnp.zeros((bp, CONV_K - 1, conv_ch), F32),
            ffn_prev=jnp.zeros((bp, CONV_K - 1, d_ff), F32),
            mem_k=mk_p.astype(BF16), mem_v=mv_p.astype(BF16), w=w, tm_mid=512, tm_ffn=1024)

        ck = cache_attn_k[l].reshape(bs, n_cache * N_HEADS, HEAD_DIM)
        cv = cache_attn_v[l].reshape(bs, n_cache * N_HEADS, HEAD_DIM)

        def sample_attn(proj, kv_f32):
            return _sample_attn(proj.reshape(bs, ss, in_width),
                                kv_f32[0].reshape(bs, ss, ATTN_WIDTH),
                                kv_f32[1].reshape(bs, ss, ATTN_WIDTH), ck, cv, pq)

        def sample_proj(x):
            return _proj_resident(x, w["g_norm1"], w["w_in"], w["g_in_cols"], tm=BAND,
                                  seq_len=BAND, **proj_args)

        ys, sk, sv, sc, sf = _layer(
            ys, seq_len=ss, proj_fn=sample_proj, attn_fn=sample_attn,
            conv_prev=cache_conv[l], ffn_prev=cache_ffn_conv[l],
            mem_k=cache_mem_k[l], mem_v=cache_mem_v[l],
            w=w, tm_mid=256, tm_ffn=1024)

        for lst, val in zip(outs, (
                pk.reshape(bp, keep, N_HEADS, HEAD_DIM), pv.reshape(bp, keep, N_HEADS, HEAD_DIM),
                pc, pf,
                mk_p.reshape(bp, n_mem, X_HEADS, X_HEAD_DIM), mv_p.reshape(bp, n_mem, X_HEADS, X_HEAD_DIM),
                sk.reshape(bs, n_cache, N_HEADS, HEAD_DIM), sv.reshape(bs, n_cache, N_HEADS, HEAD_DIM),
                sc, sf)):
            lst.append(val)

    return (yp.reshape(bp, sp, d), ys.reshape(bs, ss, d)) + tuple(jnp.stack(o) for o in outs)
```

```python
import functools
import math

import jax
import jax.numpy as jnp
import numpy as np
from jax import lax
from jax.experimental import pallas as pl
from jax.experimental.pallas import tpu as pltpu

CHUNK = 64
N_PREV_CHUNKS = 8
BAND = N_PREV_CHUNKS * CHUNK
N_HEADS = 8
HEAD_DIM = 128
ATTN_WIDTH = N_HEADS * HEAD_DIM
REL_CLIP = 128
CONV_K = 3
X_HEADS = 4
X_HEAD_DIM = 256
X_WIDTH = X_HEADS * X_HEAD_DIM
EPS = 1e-6
NEG_INF = -1e30
LOG2E = math.log2(math.e)

BF16 = jnp.bfloat16
F32 = jnp.float32

V7X_VMEM_BYTES = 64 * 1024 * 1024
F32_SUBLANES = 8

KEY_PIECE = 2 * REL_CLIP
ATTN_PIECES = BAND // KEY_PIECE + 1
assert KEY_PIECE % CHUNK == 0 and BAND % KEY_PIECE == 0
ATTN_SUB = 4


def _nbytes(shape, dtype):
    return int(np.prod(shape)) * jnp.dtype(dtype).itemsize


def _vmem_limit(pipelined, resident=(), temps=0):
    est = 2 * sum(_nbytes(s, d) for s, d in pipelined)
    est += sum(_nbytes(s, d) for s, d in resident) + temps
    return min(int(est * 1.25) + (2 << 20), V7X_VMEM_BYTES - (6 << 20))


def _rms(xf, g):
    return xf * lax.rsqrt(jnp.mean(xf * xf, axis=-1, keepdims=True) + EPS) * g


def _dot(a, b):
    return jnp.dot(a, b, preferred_element_type=F32)


def _dot_nt(a, b):
    return lax.dot_general(a, b, (((1,), (1,)), ((), ())), preferred_element_type=F32)


def _dot_tn(a, b):
    return lax.dot_general(a, b, (((0,), (0,)), ((), ())), preferred_element_type=F32)


def _proj_kernel(x_ref, gin_ref, w_ref, gcol_ref, out_ref, n_scr, *, head_dim, n_norm_tiles,
                 row_chunk):
    j = pl.program_id(1)
    tn = out_ref.shape[1]

    @pl.when(j == 0)
    def _():
        n_scr[...] = _rms(x_ref[...], gin_ref[...]).astype(BF16)

    is_norm = j < n_norm_tiles
    for c in range(n_scr.shape[0] // row_chunk):
        rows = slice(c * row_chunk, (c + 1) * row_chunk)
        acc = _dot(n_scr[rows, :], w_ref[...])
        for h in range(tn // head_dim):
            sl = slice(h * head_dim, (h + 1) * head_dim)
            a = acc[:, sl]
            r = lax.rsqrt(jnp.mean(a * a, axis=-1, keepdims=True) + EPS)
            y = a * jnp.where(is_norm, r, 1.0) * gcol_ref[:, sl]
            out_ref[rows, sl] = y.astype(out_ref.dtype)


def _proj(x, g_in, w, gcol, *, tm, tn, head_dim, norm_cols, out_dtype, name, row_chunk=256):
    t, d = x.shape
    n = w.shape[1]
    pipelined = [((tm, d), F32), ((d, tn), BF16), ((tm, tn), out_dtype)]
    kern = functools.partial(_proj_kernel, head_dim=head_dim, n_norm_tiles=norm_cols // tn,
                             row_chunk=min(row_chunk, tm))
    return pl.pallas_call(
        kern,
        grid=(t // tm, n // tn),
        in_specs=[
            pl.BlockSpec((tm, d), lambda i, j: (i, 0)),
            pl.BlockSpec((1, d), lambda i, j: (0, 0)),
            pl.BlockSpec((d, tn), lambda i, j: (0, j)),
            pl.BlockSpec((1, tn), lambda i, j: (0, j)),
        ],
        out_specs=pl.BlockSpec((tm, tn), lambda i, j: (i, j)),
        out_shape=jax.ShapeDtypeStruct((t, n), out_dtype),
        scratch_shapes=[pltpu.VMEM((tm, d), BF16)],
        compiler_params=pltpu.CompilerParams(
            dimension_semantics=("arbitrary", "arbitrary"),
            vmem_limit_bytes=_vmem_limit(pipelined, [((tm, d), BF16)],
                                         temps=3 * _nbytes((tm, tn), F32) + _nbytes((tm, d), F32))),
        name=name,
    )(x, g_in, w, gcol)


def _proj_resident_kernel(x_ref, gin_ref, w_ref, gcol_ref, out_ref, tail_ref, *, head_dim,
                          norm_cols, col_tile, row_chunk, tail_cols):
    tm = x_ref.shape[0]
    for c in range(tm // row_chunk):
        rows = slice(c * row_chunk, (c + 1) * row_chunk)
        n = _rms(x_ref[rows, :], gin_ref[...]).astype(BF16)
        for ct in range(out_ref.shape[1] // col_tile):
            acc = _dot(n, w_ref[:, ct * col_tile:(ct + 1) * col_tile])
            for h in range(col_tile // head_dim):
                sl = slice(ct * col_tile + h * head_dim, ct * col_tile + (h + 1) * head_dim)
                y = acc[:, h * head_dim:(h + 1) * head_dim]
                if sl.start < norm_cols:
                    y = _rms(y, gcol_ref[:, sl])
                out_ref[rows, sl] = y.astype(out_ref.dtype)
                if tail_cols[0] <= sl.start < tail_cols[1]:
                    tail_ref[rows, sl.start - tail_cols[0]:sl.stop - tail_cols[0]] = y


def _proj_resident(x, g_in, w, gcol, *, tm, seq_len, tail_cols, head_dim, norm_cols, out_dtype,
                   name, col_tile=1024, row_chunk=256):
    t, d = x.shape
    n = w.shape[1]
    tiles_per_seq = seq_len // tm
    n_tail = tail_cols[1] - tail_cols[0]
    const = lambda shape: pl.BlockSpec(shape, lambda i: (0, 0), pipeline_mode=pl.Buffered(1))
    kern = functools.partial(_proj_resident_kernel, head_dim=head_dim, norm_cols=norm_cols,
                             col_tile=col_tile, row_chunk=row_chunk, tail_cols=tail_cols)
    return pl.pallas_call(
        kern,
        grid=(t // tm,),
        in_specs=[pl.BlockSpec((tm, d), lambda i: (i, 0)), const((1, d)), const((d, n)),
                  const((1, n))],
        out_specs=[pl.BlockSpec((tm, n), lambda i: (i, 0)),
                   pl.BlockSpec((tm, n_tail), lambda i: (i // tiles_per_seq, 0))],
        out_shape=[jax.ShapeDtypeStruct((t, n), out_dtype),
                   jax.ShapeDtypeStruct((t // seq_len * tm, n_tail), F32)],
        compiler_params=pltpu.CompilerParams(
            dimension_semantics=("arbitrary",),
            vmem_limit_bytes=_vmem_limit(
                [((tm, d), F32), ((tm, n), out_dtype), ((tm, n_tail), F32)], [((d, n), BF16)],
                temps=_nbytes((row_chunk, d), F32) + 3 * _nbytes((row_chunk, col_tile), F32))),
        name=name,
    )(x, g_in, w, gcol)


def _toeplitz(vec, rows):
    x = jnp.broadcast_to(vec, (rows, KEY_PIECE))
    r = lax.broadcasted_iota(jnp.int32, (rows, KEY_PIECE), 0)
    for b in range((rows - 1).bit_length()):
        x = jnp.where((r >> b) & 1 == 1, pltpu.roll(x, 1 << b, 1), x)
    return x


def _build_band_bias(pq_ref, bias_scr, mask_band, keys_major=False):
    rows = bias_scr.shape[2] if keys_major else bias_scr.shape[1]
    r = lax.broadcasted_iota(jnp.int32, (rows, KEY_PIECE), 0)
    c = lax.broadcasted_iota(jnp.int32, (rows, KEY_PIECE), 1)
    upper = c >= r
    for h in range(N_HEADS):
        top = jnp.broadcast_to(pq_ref[h, 0:1, :], (rows, KEY_PIECE))
        tp = _toeplitz(pq_ref[h, 1:2, :], rows)
        tq = _toeplitz(pq_ref[h, 2:3, :], rows)
        pieces = [top] * (ATTN_PIECES - 2) + [jnp.where(upper, tp, top), jnp.where(upper, tq, tp)]
        for w, piece in enumerate(pieces):
            piece = piece * LOG2E
            if mask_band:
                kc = w * (KEY_PIECE // CHUNK) + c // CHUNK
                qc = r // CHUNK
                piece = jnp.where((kc >= qc) & (kc <= qc + N_PREV_CHUNKS), piece, NEG_INF)
            if keys_major:
                bias_scr[h, w * KEY_PIECE:(w + 1) * KEY_PIECE, :] = piece.T
            else:
                bias_scr[h, :, w * KEY_PIECE:(w + 1) * KEY_PIECE] = piece


def _rel_rows(table):
    h = table.shape[0]
    top = table[:, 2 * REL_CLIP:]
    lo = table[:, :1]
    p = jnp.concatenate([jnp.broadcast_to(top, (h, REL_CLIP + 1)),
                         table[:, 2 * REL_CLIP - 1:REL_CLIP:-1]], axis=1)
    q = jnp.concatenate([table[:, REL_CLIP::-1],
                         jnp.broadcast_to(lo, (h, KEY_PIECE - REL_CLIP - 1))], axis=1)
    return jnp.stack([jnp.broadcast_to(top, (h, KEY_PIECE)), p, q], axis=1).astype(F32)


def _softmax_pv(scores, values):
    widths = {s.shape[1] for s in scores}
    if len(widths) == 1:
        m = jnp.max(functools.reduce(jnp.maximum, scores), axis=-1, keepdims=True)
    else:
        m = functools.reduce(jnp.maximum, [jnp.max(s, axis=-1, keepdims=True) for s in scores])
    ps = [jnp.exp2(s - m) for s in scores]
    if len(widths) == 1:
        l = jnp.sum(functools.reduce(jnp.add, ps), axis=-1, keepdims=True)
    else:
        l = functools.reduce(jnp.add, [jnp.sum(p, axis=-1, keepdims=True) for p in ps])
    o = functools.reduce(jnp.add, [_dot(p.astype(BF16), v) for p, v in zip(ps, values)])
    return o * (1.0 / l)


def _prompt_attn_kernel(pq_ref, q_ref, *rest):
    n_blk = ATTN_PIECES - 1 + ATTN_SUB
    k_refs = rest[:n_blk]
    v_refs = rest[n_blk:2 * n_blk]
    o_ref, bias_scr = rest[2 * n_blk:]
    t = pl.program_id(1)
    qk_scale = HEAD_DIM ** -0.5 * LOG2E

    @pl.when((pl.program_id(0) == 0) & (t == 0))
    def _():
        _build_band_bias(pq_ref, bias_scr, mask_band=True, keys_major=True)

    def attend(seq_start):
        ones = jnp.ones((KEY_PIECE, HEAD_DIM), BF16)
        for u in range(ATTN_SUB):
            first_piece = max(0, ATTN_PIECES - 1 - u) if seq_start else 0
            pieces = range(first_piece, ATTN_PIECES)
            for h in range(N_HEADS):
                sl = slice(h * HEAD_DIM, (h + 1) * HEAD_DIM)
                q = q_ref[0, u * KEY_PIECE:(u + 1) * KEY_PIECE, sl]
                st = [_dot_nt(k_refs[u + w][0, :, sl], q) * qk_scale
                      + bias_scr[h, w * KEY_PIECE:(w + 1) * KEY_PIECE, :] for w in pieces]
                m = jnp.max(functools.reduce(jnp.maximum, st), axis=0, keepdims=True)
                o = None
                for w, s in zip(pieces, st):
                    v_ones = jnp.concatenate([v_refs[u + w][0, :, sl], ones], axis=1)
                    part = _dot_tn(jnp.exp2(s - m).astype(BF16), v_ones)
                    o = part if o is None else o + part
                o_ref[0, u * KEY_PIECE:(u + 1) * KEY_PIECE, sl] = (
                    o[:, :HEAD_DIM] / o[:, HEAD_DIM:]).astype(o_ref.dtype)

    pl.when(t == 0)(functools.partial(attend, True))
    pl.when(t > 0)(functools.partial(attend, False))


def _prompt_attn(proj, pq, *, batch, seq):
    kp = KEY_PIECE
    tq = ATTN_SUB * kp
    n_blk = ATTN_PIECES - 1 + ATTN_SUB

    def kv_spec(p, col):
        return pl.BlockSpec(
            (1, kp, ATTN_WIDTH),
            lambda b, t: (b, jnp.maximum(t * ATTN_SUB - (ATTN_PIECES - 1) + p, 0), col))

    blk = ((1, kp, ATTN_WIDTH), BF16)
    qblk = ((1, tq, ATTN_WIDTH), BF16)
    bias_shape = (N_HEADS, ATTN_PIECES * kp, kp)
    return pl.pallas_call(
        _prompt_attn_kernel,
        grid=(batch, seq // tq),
        in_specs=[pl.BlockSpec(pq.shape, lambda b, t: (0, 0, 0)),
                  pl.BlockSpec((1, tq, ATTN_WIDTH), lambda b, t: (b, t, 0))]
        + [kv_spec(p, 1) for p in range(n_blk)]
        + [kv_spec(p, 2) for p in range(n_blk)],
        out_specs=pl.BlockSpec((1, tq, ATTN_WIDTH), lambda b, t: (b, t, 0)),
        out_shape=jax.ShapeDtypeStruct((batch, seq, ATTN_WIDTH), BF16),
        scratch_shapes=[pltpu.VMEM(bias_shape, F32)],
        compiler_params=pltpu.CompilerParams(
            dimension_semantics=("arbitrary", "arbitrary"),
            vmem_limit_bytes=_vmem_limit([blk] * (2 * n_blk) + [qblk] * 2, [(bias_shape, F32)],
                                         temps=16 * _nbytes((kp, kp), F32))),
        name="prompt_band_attn",
    )(pq, proj, *([proj] * (2 * n_blk)))


def _sample_attn_kernel(pq_ref, q_ref, kn_ref, vn_ref, kf_ref, vf_ref, ck_ref, cv_ref,
                        o_ref, sk_ref, sv_ref, bias_scr):
    qk_scale = HEAD_DIM ** -0.5 * LOG2E
    t_new = q_ref.shape[1]
    n_cache = ck_ref.shape[1] // N_HEADS
    keep = (n_cache - t_new) * N_HEADS

    @pl.when(pl.program_id(0) == 0)
    def _():
        _build_band_bias(pq_ref, bias_scr, mask_band=False)

    for h in range(N_HEADS):
        sl = slice(h * HEAD_DIM, (h + 1) * HEAD_DIM)
        head_rows = pl.ds(h, n_cache, stride=N_HEADS)
        q = q_ref[0, :, sl]
        s_c = _dot_nt(q, ck_ref[0, head_rows, :].astype(BF16)) * qk_scale + bias_scr[h, :, :n_cache]
        s_n = _dot_nt(q, kn_ref[0, :, sl]) * qk_scale + bias_scr[h, :, n_cache:n_cache + t_new]
        o = _softmax_pv([s_c, s_n], [cv_ref[0, head_rows, :].astype(BF16), vn_ref[0, :, sl]])
        o_ref[0, :, sl] = o.astype(o_ref.dtype)
        new_rows = pl.ds(keep + h, t_new, stride=N_HEADS)
        sk_ref[0, new_rows, :] = kf_ref[0, :, sl]
        sv_ref[0, new_rows, :] = vf_ref[0, :, sl]
    sk_ref[0, :keep, :] = ck_ref[0, t_new * N_HEADS:, :]
    sv_ref[0, :keep, :] = cv_ref[0, t_new * N_HEADS:, :]


def _sample_attn(proj, kv_f32, cache_k, cache_v, pq):
    b, t, _ = proj.shape
    lh = cache_k.shape[1]
    assert lh == BAND * N_HEADS and t <= KEY_PIECE
    new_bf = ((1, t, ATTN_WIDTH), BF16)
    new_f = ((1, t, ATTN_WIDTH), F32)
    cache_blk = ((1, lh, HEAD_DIM), F32)
    bias_shape = (N_HEADS, t, ATTN_PIECES * KEY_PIECE)

    def new_spec(col):
        return pl.BlockSpec((1, t, ATTN_WIDTH), lambda i: (i, 0, col))

    cache_spec = pl.BlockSpec((1, lh, HEAD_DIM), lambda i: (i, 0, 0))
    return pl.pallas_call(
        _sample_attn_kernel,
        grid=(b,),
        in_specs=[pl.BlockSpec(pq.shape, lambda i: (0, 0, 0)),
                  new_spec(0), new_spec(1), new_spec(2), new_spec(0), new_spec(1),
                  cache_spec, cache_spec],
        out_specs=[new_spec(0), cache_spec, cache_spec],
        out_shape=[jax.ShapeDtypeStruct((b, t, ATTN_WIDTH), BF16),
                   jax.ShapeDtypeStruct((b, lh, HEAD_DIM), F32),
                   jax.ShapeDtypeStruct((b, lh, HEAD_DIM), F32)],
        scratch_shapes=[pltpu.VMEM(bias_shape, F32)],
        compiler_params=pltpu.CompilerParams(
            dimension_semantics=("arbitrary",),
            vmem_limit_bytes=_vmem_limit(
                [new_bf] * 4 + [new_f] * 2 + [cache_blk] * 4, [(bias_shape, F32)],
                temps=8 * _nbytes((t, BAND), F32) + 2 * _nbytes((BAND, HEAD_DIM), F32))),
        name="sample_band_attn",
    )(pq, proj, proj, proj, kv_f32, kv_f32, cache_k, cache_v)


def _causal_dwconv3(x, prev, w_ref, nseg, lseg):
    c = x.shape[1]
    p0 = prev[:, 0:1, :]
    p1 = prev[:, 1:2, :]
    x3 = x.reshape(nseg, lseg, c)
    sh1 = pltpu.roll(x, 1, 0).reshape(nseg, lseg, c)
    sh2 = pltpu.roll(x, 2, 0).reshape(nseg, lseg, c)
    pos = lax.broadcasted_iota(jnp.int32, (nseg, F32_SUBLANES, c), 1)
    top1 = jnp.where(pos == 0, p1, sh1[:, :F32_SUBLANES])
    top2 = jnp.where(pos == 0, p0, jnp.where(pos == 1, p1, sh2[:, :F32_SUBLANES]))
    sh1 = jnp.concatenate([top1, sh1[:, F32_SUBLANES:]], axis=1)
    sh2 = jnp.concatenate([top2, sh2[:, F32_SUBLANES:]], axis=1)
    y = sh2 * w_ref[0:1, :] + sh1 * w_ref[1:2, :] + x3 * w_ref[2:3, :]
    return y.reshape(nseg * lseg, c), x3[:, lseg - (CONV_K - 1):, :]


def _conv_prev(c, row_chunk, lseg, tiles_per_seq, tile_idx, prev_ref, carry, tail):
    if (c * row_chunk) % lseg != 0:
        return tail[None]
    if tiles_per_seq > 1:
        return jnp.where(tile_idx % tiles_per_seq == 0, prev_ref[...], carry[None])
    s0 = c * row_chunk // lseg
    return prev_ref[s0:s0 + max(1, row_chunk // lseg)]


def _mid_kernel(a_ref, b_ref, c_ref, u_ref, prev_ref, x_ref, mk_ref, mv_ref,
                wconv_ref, ga_ref, gc_ref, wout_ref, g2_ref, wxq_ref, gxq_ref, wxo_ref,
                h_ref, newc_ref, mixed_scr, o_scr, carry_scr, *, lseg, tiles_per_seq, row_chunk):
    i = pl.program_id(0)
    tm = x_ref.shape[0]
    seg_per_chunk = max(1, row_chunk // lseg)
    lseg_c = min(row_chunk, lseg)
    sm_scale = X_HEAD_DIM ** -0.5 * LOG2E
    n_chunks = tm // row_chunk
    state = [dict() for _ in range(n_chunks)]

    def rows_of(c):
        return slice(c * row_chunk, (c + 1) * row_chunk)

    def mix(c):
        rows, st = rows_of(c), state[c]
        s0 = c * row_chunk // lseg
        mixed_scr[rows, :ATTN_WIDTH] = _rms(a_ref[rows, :].astype(F32), ga_ref[...]).astype(BF16)
        cu = c_ref[rows, :].astype(F32) * u_ref[rows, :].astype(F32)
        tail = state[c - 1]["tail"] if c > 0 else None
        prev = _conv_prev(c, row_chunk, lseg, tiles_per_seq, i, prev_ref, carry_scr[...], tail)
        st["tail"] = cu[row_chunk - (CONV_K - 1):, :]
        conv, new_conv = _causal_dwconv3(cu, prev, wconv_ref, seg_per_chunk, lseg_c)
        if ((c + 1) * row_chunk) % lseg == 0:
            newc_ref[s0:s0 + seg_per_chunk] = new_conv
        mixed_scr[rows, ATTN_WIDTH:] = _rms(b_ref[rows, :].astype(F32) * conv,
                                            gc_ref[...]).astype(BF16)

    def out_proj(c):
        rows = rows_of(c)
        h_ref[rows, :] = x_ref[rows, :] + _dot(mixed_scr[rows, :], wout_ref[...])

    def q_proj(c):
        rows, st = rows_of(c), state[c]
        st["qx"] = _dot(_rms(h_ref[rows, :], g2_ref[...]).astype(BF16), wxq_ref[...])

    def cross_attn(c):
        st = state[c]
        s0 = c * row_chunk // lseg
        heads_major = len(mk_ref.shape) == 4
        if heads_major:
            mem = [(jnp.swapaxes(mk_ref[s0 + s], 0, 1).astype(BF16),
                    jnp.swapaxes(mv_ref[s0 + s], 0, 1).astype(BF16))
                   for s in range(seg_per_chunk)]
        for hd in range(X_HEADS):
            sl = slice(hd * X_HEAD_DIM, (hd + 1) * X_HEAD_DIM)
            qh = _rms(st["qx"][:, sl], gxq_ref[...]).astype(BF16)
            for s in range(seg_per_chunk):
                srows = slice(s * lseg_c, (s + 1) * lseg_c)
                if heads_major:
                    mk, mv = mem[s][0][hd], mem[s][1][hd]
                else:
                    mk, mv = mk_ref[s0 + s, :, sl].astype(BF16), mv_ref[s0 + s, :, sl].astype(BF16)
                sc = _dot_nt(qh[srows], mk) * sm_scale
                o = _softmax_pv([sc], [mv])
                o_scr[c * row_chunk + s * lseg_c:c * row_chunk + (s + 1) * lseg_c, sl] = o.astype(BF16)

    def x_out(c):
        rows = rows_of(c)
        h_ref[rows, :] += _dot(o_scr[rows, :], wxo_ref[...])

    stages = (mix, out_proj, q_proj, cross_attn, x_out)
    for step in range(n_chunks + len(stages) - 1):
        for k, stage in enumerate(stages):
            c = step - k
            if 0 <= c < n_chunks:
                stage(c)
    if tiles_per_seq > 1:
        carry_scr[...] = state[-1]["tail"]


def _mid(a, proj, conv_prev, x, mem_k, mem_v, w_conv, g_a, g_c, w_out, g2, w_xq, g_xq, w_xo,
         *, tm, seq_len, row_chunk=256):
    t, d = x.shape
    cw = conv_prev.shape[2]
    n_mem = mem_k.shape[1]
    lseg = min(tm, seq_len)
    nseg = tm // lseg
    tps = max(1, seq_len // tm)
    row_chunk = min(row_chunk, tm)
    assert lseg % row_chunk == 0 or row_chunk % lseg == 0

    def col_spec(col):
        return pl.BlockSpec((tm, cw), lambda i: (i, col))

    def const_spec(arr):
        return pl.BlockSpec(arr.shape, lambda i: (0,) * arr.ndim, pipeline_mode=pl.Buffered(1))

    seq_spec3 = lambda rows, width: pl.BlockSpec((nseg, rows, width), lambda i: (i // tps, 0, 0))
    kern = functools.partial(_mid_kernel, lseg=lseg, tiles_per_seq=tps, row_chunk=row_chunk)
    pipelined = ([((tm, cw), BF16)] * 4 + [((nseg, 2, cw), F32)] * 2 + [((tm, d), F32)] * 2
                 + [((nseg, n_mem, X_WIDTH), mem_k.dtype)] * 2)
    mem_spec = pl.BlockSpec((nseg,) + mem_k.shape[1:],
                            lambda i: (i // tps,) + (0,) * (mem_k.ndim - 1))
    resident = [(w.shape, w.dtype) for w in (w_out, w_xq, w_xo)]
    resident += [((tm, d), BF16), ((tm, X_WIDTH), BF16)]
    return pl.pallas_call(
        kern,
        grid=(t // tm,),
        in_specs=[
            pl.BlockSpec((tm, ATTN_WIDTH), lambda i: (i, 0)),
            col_spec(3), col_spec(4), col_spec(5),
            seq_spec3(2, cw),
            pl.BlockSpec((tm, d), lambda i: (i, 0)),
            mem_spec, mem_spec,
            const_spec(w_conv), const_spec(g_a), const_spec(g_c), const_spec(w_out),
            const_spec(g2), const_spec(w_xq), const_spec(g_xq), const_spec(w_xo),
        ],
        out_specs=[pl.BlockSpec((tm, d), lambda i: (i, 0)), seq_spec3(2, cw)],
        out_shape=[jax.ShapeDtypeStruct((t, d), F32),
                   jax.ShapeDtypeStruct(conv_prev.shape, F32)],
        scratch_shapes=[pltpu.VMEM((tm, d), BF16), pltpu.VMEM((tm, X_WIDTH), BF16),
                        pltpu.VMEM((CONV_K - 1, cw), F32)],
        compiler_params=pltpu.CompilerParams(
            dimension_semantics=("arbitrary",),
            vmem_limit_bytes=_vmem_limit(pipelined, resident,
                                         temps=8 * _nbytes((row_chunk, d), F32))),
        name="mid",
    )(a, proj, proj, proj, conv_prev, x, mem_k, mem_v,
      w_conv, g_a, g_c, w_out, g2, w_xq, g_xq, w_xo)


def _ffn_kernel(h_ref, g3_ref, wup_ref, wgate_ref, wconv_ref, wdown_ref, prev_ref,
                y_ref, newf_ref, n3_scr, carry_scr, *, lseg, tiles_per_seq, row_chunk,
                first_row_chunk):
    i = pl.program_id(0)
    f = pl.program_id(1)
    tm = h_ref.shape[0]

    def body(first, row_chunk):
        seg_per_chunk = max(1, row_chunk // lseg)
        lseg_c = min(row_chunk, lseg)
        tail = None
        for c in range(tm // row_chunk):
            rows = slice(c * row_chunk, (c + 1) * row_chunk)
            if first:
                h = h_ref[rows, :]
                n3 = _rms(h, g3_ref[...]).astype(BF16)
                n3_scr[rows, :] = n3
            else:
                n3 = n3_scr[rows, :]
            up = _dot(n3, wup_ref[...])
            gate = _dot(n3, wgate_ref[...])
            prev = _conv_prev(c, row_chunk, lseg, tiles_per_seq, i, prev_ref, carry_scr[f], tail)
            tail = up[row_chunk - (CONV_K - 1):, :]
            conv, new_ffn = _causal_dwconv3(up, prev, wconv_ref, seg_per_chunk, lseg_c)
            if ((c + 1) * row_chunk) % lseg == 0:
                s0 = c * row_chunk // lseg
                newf_ref[f, s0:s0 + seg_per_chunk] = new_ffn
            act = conv / (1.0 + jnp.exp(-conv)) * gate
            down = _dot(act.astype(BF16), wdown_ref[...])
            if first:
                y_ref[rows, :] = h + down
            else:
                y_ref[rows, :] += down
        if tiles_per_seq > 1:
            carry_scr[f] = tail

    pl.when(f == 0)(functools.partial(body, True, first_row_chunk))
    pl.when(f > 0)(functools.partial(body, False, row_chunk))


def _ffn(h, g3, w_up, w_gate, w_conv, w_down, ffn_prev, *, tm, tf, seq_len, row_chunk=512):
    t, d = h.shape
    dff = w_up.shape[1]
    n_seq = ffn_prev.shape[0]
    lseg = min(tm, seq_len)
    nseg = tm // lseg
    tps = max(1, seq_len // tm)
    nf = dff // tf
    row_chunk = min(row_chunk, tm)
    assert lseg % row_chunk == 0 or row_chunk % lseg == 0
    first_row_chunk = min(row_chunk, 256)
    assert lseg % first_row_chunk == 0 or first_row_chunk % lseg == 0
    kern = functools.partial(_ffn_kernel, lseg=lseg, tiles_per_seq=tps, row_chunk=row_chunk,
                             first_row_chunk=first_row_chunk)
    prev_spec = pl.BlockSpec((nseg, CONV_K - 1, tf), lambda i, f: (i // tps, 0, f))
    newf_blk = (nf, nseg, CONV_K - 1, tf)
    pipelined = ([((tm, d), F32)] * 2 + [((d, tf), BF16)] * 3 + [((nseg, 2, tf), F32)]
                 + [(newf_blk, F32), ((CONV_K, tf), F32)])
    resident = [((tm, d), BF16), ((nf, CONV_K - 1, tf), F32)]
    y, new_ffn = pl.pallas_call(
        kern,
        grid=(t // tm, nf),
        in_specs=[
            pl.BlockSpec((tm, d), lambda i, f: (i, 0)),
            pl.BlockSpec((1, d), lambda i, f: (0, 0)),
            pl.BlockSpec((d, tf), lambda i, f: (0, f)),
            pl.BlockSpec((d, tf), lambda i, f: (0, f)),
            pl.BlockSpec((CONV_K, tf), lambda i, f: (0, f)),
            pl.BlockSpec((tf, d), lambda i, f: (f, 0)),
            prev_spec,
        ],
        out_specs=[pl.BlockSpec((tm, d), lambda i, f: (i, 0)),
                   pl.BlockSpec(newf_blk, lambda i, f: (0, i // tps, 0, 0))],
        out_shape=[jax.ShapeDtypeStruct((t, d), F32),
                   jax.ShapeDtypeStruct((nf, n_seq, CONV_K - 1, tf), F32)],
        scratch_shapes=[pltpu.VMEM((tm, d), BF16), pltpu.VMEM((nf, CONV_K - 1, tf), F32)],
        compiler_params=pltpu.CompilerParams(
            dimension_semantics=("arbitrary", "arbitrary"),
            vmem_limit_bytes=_vmem_limit(pipelined, resident, temps=10 * _nbytes((tm, tf), F32))),
        name="ffn",
    )(h, g3, w_up, w_gate, w_conv, w_down, ffn_prev)
    return y, new_ffn.transpose(1, 2, 0, 3).reshape(n_seq, CONV_K - 1, dff)


def _layer(x, *, seq_len, proj_fn, attn_fn, conv_prev, ffn_prev, mem_k, mem_v, w, tm_mid, tm_ffn):
    proj, kv_f32 = proj_fn(x)
    a, new_k, new_v = attn_fn(proj, kv_f32)
    h, new_conv = _mid(a.reshape(-1, ATTN_WIDTH), proj, conv_prev, x, mem_k, mem_v,
                       w["w_conv_mix"], w["g_out_attn"], w["g_out_conv"], w["w_out"],
                       w["g_norm2"], w["w_xq"], w["g_xq"], w["w_xo"], tm=tm_mid, seq_len=seq_len)
    y, new_ffn = _ffn(h, w["g_norm3"], w["w_up"], w["w_gate"], w["w_ffn_conv"], w["w_down"],
                      ffn_prev, tm=tm_ffn, tf=512, seq_len=seq_len)
    return y, new_k, new_v, new_conv, new_ffn


def kernel(x_prompt, x_sample, mem_prompt, cache_attn_k, cache_attn_v, cache_conv, cache_ffn_conv, cache_mem_k, cache_mem_v, g_norm1, w_in, g_q, g_k, rel_bias, w_conv_mix, g_out_attn, g_out_conv, w_out, g_norm2, g_mem_norm, w_xq, w_xkv, g_xq, g_xk, w_xo, g_norm3, w_up, w_gate, w_ffn_conv, w_down):
    bp, sp, d = x_prompt.shape
    bs, ss, _ = x_sample.shape
    depth = w_in.shape[0]
    in_width = w_in.shape[2]
    conv_ch = w_conv_mix.shape[2]
    d_ff = w_up.shape[2]
    n_mem = mem_prompt.shape[1]
    n_cache = cache_attn_k.shape[2]
    keep = min(BAND, sp)

    yp = x_prompt.reshape(bp * sp, d)
    ys = x_sample.reshape(bs * ss, d)
    outs = [[] for _ in range(10)]
    for l in range(depth):
        row = lambda g: g[l][None, :]
        w = dict(
            g_norm1=row(g_norm1), w_in=w_in[l].astype(BF16),
            g_in_cols=jnp.concatenate([jnp.tile(g_q[l], N_HEADS), jnp.tile(g_k[l], N_HEADS),
                                       jnp.ones((in_width - 2 * ATTN_WIDTH,), F32)])[None, :],
            w_conv_mix=w_conv_mix[l], g_out_attn=row(g_out_attn), g_out_conv=row(g_out_conv),
            w_out=w_out[l].astype(BF16), g_norm2=row(g_norm2), w_xq=w_xq[l].astype(BF16),
            g_xq=row(g_xq), w_xo=w_xo[l].astype(BF16), g_norm3=row(g_norm3),
            w_up=w_up[l].astype(BF16), w_gate=w_gate[l].astype(BF16),
            w_ffn_conv=w_ffn_conv[l], w_down=w_down[l].astype(BF16))
        pq = _rel_rows(rel_bias[l])

        g_mem_cols = jnp.concatenate([jnp.tile(g_xk[l], X_HEADS), jnp.ones((X_WIDTH,), F32)])[None, :]
        mem_kv = _proj(mem_prompt.reshape(bp * n_mem, d), row(g_mem_norm), w_xkv[l].astype(BF16),
                       g_mem_cols, tm=bp * n_mem, tn=512, head_dim=X_HEAD_DIM,
                       norm_cols=X_WIDTH, out_dtype=F32, name="mem_kv")
        mem_kv = mem_kv.reshape(bp, n_mem, 2 * X_WIDTH)
        mk_p, mv_p = mem_kv[:, :, :X_WIDTH], mem_kv[:, :, X_WIDTH:]

        proj_args = dict(head_dim=HEAD_DIM, norm_cols=2 * ATTN_WIDTH)
        kv_cols = (ATTN_WIDTH, 3 * ATTN_WIDTH)

        def prompt_proj(x):
            return _proj_resident(x, w["g_norm1"], w["w_in"], w["g_in_cols"], tm=keep, seq_len=sp,
                                  tail_cols=kv_cols, out_dtype=BF16, name="in_proj", **proj_args)

        def prompt_attn(proj, kv_tail):
            a = _prompt_attn(proj.reshape(bp, sp, in_width), pq, batch=bp, seq=sp)
            kv = kv_tail.reshape(bp, keep, 2 * ATTN_WIDTH)
            return a, kv[:, :, :ATTN_WIDTH], kv[:, :, ATTN_WIDTH:]

        yp, pk, pv, pc, pf = _layer(
            yp, seq_len=sp, proj_fn=prompt_proj, attn_fn=prompt_attn,
            conv_prev=jnp.zeros((bp, CONV_K - 1, conv_ch), F32),
            ffn_prev=jnp.zeros((bp, CONV_K - 1, d_ff), F32),
            mem_k=mk_p.astype(BF16), mem_v=mv_p.astype(BF16), w=w, tm_mid=512, tm_ffn=1024)

        ck = cache_attn_k[l].reshape(bs, n_cache * N_HEADS, HEAD_DIM)
        cv = cache_attn_v[l].reshape(bs, n_cache * N_HEADS, HEAD_DIM)

        def sample_attn(proj, kv_f32):
            return _sample_attn(proj.reshape(bs, ss, in_width),
                                kv_f32.reshape(bs, ss, 2 * ATTN_WIDTH), ck, cv, pq)

        def sample_proj(x):
            return _proj_resident(x, w["g_norm1"], w["w_in"], w["g_in_cols"], tm=BAND,
                                  seq_len=BAND, tail_cols=kv_cols, out_dtype=BF16,
                                  name="in_proj", **proj_args)

        ys, sk, sv, sc, sf = _layer(
            ys, seq_len=ss, proj_fn=sample_proj, attn_fn=sample_attn,
            conv_prev=cache_conv[l], ffn_prev=cache_ffn_conv[l],
            mem_k=cache_mem_k[l], mem_v=cache_mem_v[l],
            w=w, tm_mid=256, tm_ffn=512)

        for lst, val in zip(outs, (
                pk.reshape(bp, keep, N_HEADS, HEAD_DIM), pv.reshape(bp, keep, N_HEADS, HEAD_DIM),
                pc, pf,
                mk_p.reshape(bp, n_mem, X_HEADS, X_HEAD_DIM), mv_p.reshape(bp, n_mem, X_HEADS, X_HEAD_DIM),
                sk.reshape(bs, n_cache, N_HEADS, HEAD_DIM), sv.reshape(bs, n_cache, N_HEADS, HEAD_DIM),
                sc, sf)):
            lst.append(val)

    return (yp.reshape(bp, sp, d), ys.reshape(bs, ss, d)) + tuple(jnp.stack(o) for o in outs)
```
